```python
import math
import jax, jax.numpy as jnp
from jax import lax
import numpy as np

D_MODEL = 2048
BATCH = 4
SEQ = 2048
DEPTH = 4
DEC_BATCH = 8
DEC_SEQ = 1
PAST_LEN = 16384
PAGE_SIZE = 128

HEAD_DIM = 64
D_SSM = D_MODEL // 2
D_ATT = D_MODEL - D_SSM
D_MIX = D_SSM + D_ATT
H_SSM = D_SSM // HEAD_DIM
H_ATT = D_ATT // HEAD_DIM
N_BC_GROUPS = 2
D_STATE = 128
CONV_W = 4
CONV_DIM = D_SSM + 2 * N_BC_GROUPS * D_STATE
SSD_CHUNK = 128
SB_BLOCK = 128
SB_BIAS_INIT = -5.0
D_FF = 11 * D_MODEL // 4
PLE_DIM = 256
RMS_EPS = 1e-6
OFF_Z = 0
OFF_XBC = OFF_Z + D_SSM
OFF_DT = OFF_XBC + CONV_DIM
OFF_Q = OFF_DT + H_SSM
OFF_K = OFF_Q + D_ATT
OFF_V = OFF_K + D_ATT
N_IN = OFF_V + D_ATT

kernel_name = 'hymba_ssd_stickbreak_macaron_step'


def rms_norm(x, g):
    xf = x.astype(jnp.float32)
    y = xf * lax.rsqrt(jnp.mean(xf * xf, axis=-1, keepdims=True) + RMS_EPS) * g.astype(jnp.float32)
    return y.astype(x.dtype)


def swiglu_ffn(u, w_gu, w_down):
    gu = u @ w_gu
    g, up = gu[..., :D_FF], gu[..., D_FF:]
    return (jax.nn.silu(g) * up) @ w_down


def causal_conv(xbc, conv_buf, w, b):
    full = jnp.concatenate([conv_buf.astype(xbc.dtype), xbc], axis=1)
    T = xbc.shape[1]
    out = b
    for j in range(CONV_W):
        out = out + full[:, j:j + T] * w[j]
    return jax.nn.silu(out), full[:, -(CONV_W - 1):]


def ssd_scan(x, dt, A, B, C, h0):
    b, T, H, P = x.shape
    G, N = B.shape[2], B.shape[3]
    K = H // G
    L = SSD_CHUNK if T % SSD_CHUNK == 0 else T
    nc = T // L
    x = x.reshape(b, nc, L, G, K, P)
    dt = dt.reshape(b, nc, L, G, K)
    B = B.reshape(b, nc, L, G, N)
    C = C.reshape(b, nc, L, G, N)
    xdt = x * dt[..., None]
    cum = jnp.cumsum(dt * A.reshape(G, K), axis=2)
    tri = jnp.tril(jnp.ones((L, L), dtype=bool))[:, :, None, None]
    diff = cum[:, :, :, None] - cum[:, :, None, :]
    decay = jnp.exp(jnp.where(tri, diff, -jnp.inf))
    cb = jnp.einsum('bctgn,bcsgn->bctsg', C, B)
    y_diag = jnp.einsum('bctsg,bctsgk,bcsgkp->bctgkp', cb, decay, xdt)
    decay_to_end = jnp.exp(cum[:, :, -1:] - cum)
    chunk_states = jnp.einsum('bcsgn,bcsgk,bcsgkp->bcgkpn', B, decay_to_end, xdt)
    chunk_decay = jnp.exp(cum[:, :, -1])

    def step(h, inp):
        dec, st = inp
        return dec[..., None, None] * h + st, h

    h_final, h_in = lax.scan(step, h0.reshape(b, G, K, P, N),
                             (jnp.moveaxis(chunk_decay, 1, 0), jnp.moveaxis(chunk_states, 1, 0)))
    h_in = jnp.moveaxis(h_in, 0, 1)
    y_off = jnp.einsum('bctgn,bcgkpn,bctgk->bctgkp', C, h_in, jnp.exp(cum))
    y = (y_diag + y_off).reshape(b, T, H, P)
    return y, h_final.reshape(b, H, P, N)


def stick_breaking_attention(q, k, v, bias, q_offset):
    b, T, H, d = q.shape
    S = k.shape[1]
    blk = SB_BLOCK if T % SB_BLOCK == 0 else T
    nb = T // blk
    scale = 1.0 / math.sqrt(d)
    qb = jnp.moveaxis(q.astype(jnp.float32).reshape(b, nb, blk, H, d), 1, 0)
    qpos = (q_offset + jnp.arange(T, dtype=jnp.int32)).reshape(nb, blk)
    kpos = jnp.arange(S, dtype=jnp.int32)
    kf = k.astype(jnp.float32)
    vf = v.astype(jnp.float32)
    bf = bias.astype(jnp.float32)[None, :, None, None]

    def one_block(args):
        qblk, pos = args
        z = jnp.einsum('bqhd,bshd->bhqs', qblk, kf) * scale + bf
        mask = kpos[None, :] < pos[:, None]
        log_keep = jnp.where(mask, jax.nn.log_sigmoid(-z), 0.0)
        tail = lax.cumsum(log_keep, axis=3, reverse=True) - log_keep
        a = jnp.where(mask, jnp.exp(jax.nn.log_sigmoid(z) + tail), 0.0)
        return jnp.einsum('bhqs,bshd->bqhd', a, vf)

    out = lax.map(one_block, (qb, qpos))
    return jnp.moveaxis(out, 0, 1).reshape(b, T, H, d)


def decoder_layer(h, p, conv_buf, ssm_h0, k_past, v_past, W):
    b, T, _ = h.shape
    h = h + 0.5 * rms_norm(swiglu_ffn(rms_norm(h, W['ffn1_pre_g']), W['ffn1_w_gu'], W['ffn1_w_down']), W['ffn1_post_g'])
    u = rms_norm(h, W['mix_pre_g'])
    proj = u @ W['w_in']
    z = proj[..., OFF_Z:OFF_XBC]
    xbc = proj[..., OFF_XBC:OFF_DT]
    dt_raw = proj[..., OFF_DT:OFF_Q]
    q = proj[..., OFF_Q:OFF_K].reshape(b, T, H_ATT, HEAD_DIM)
    k = proj[..., OFF_K:OFF_V].reshape(b, T, H_ATT, HEAD_DIM)
    v = proj[..., OFF_V:N_IN].reshape(b, T, H_ATT, HEAD_DIM)
    xbc_act, new_conv = causal_conv(xbc, conv_buf, W['conv_w'], W['conv_b'])
    xbc_act = xbc_act.astype(jnp.float32)
    xs = xbc_act[..., :D_SSM].reshape(b, T, H_SSM, HEAD_DIM)
    Bs = xbc_act[..., D_SSM:D_SSM + N_BC_GROUPS * D_STATE].reshape(b, T, N_BC_GROUPS, D_STATE)
    Cs = xbc_act[..., D_SSM + N_BC_GROUPS * D_STATE:].reshape(b, T, N_BC_GROUPS, D_STATE)
    dt = jax.nn.softplus(dt_raw.astype(jnp.float32) + W['dt_bias'].astype(jnp.float32))
    A = -jnp.exp(W['a_log'].astype(jnp.float32))
    y, h_new = ssd_scan(xs, dt, A, Bs, Cs, ssm_h0.astype(jnp.float32))
    y = (y + xs * W['d_skip'].astype(jnp.float32)[:, None]).reshape(b, T, D_SSM)
    y_ssd = rms_norm(y * jax.nn.silu(z.astype(jnp.float32)), W['ssm_norm_g']).astype(h.dtype)
    if k_past is None:
        keys, vals, past = k, v, 0
    else:
        keys = jnp.concatenate([k_past.astype(k.dtype), k], axis=1)
        vals = jnp.concatenate([v_past.astype(v.dtype), v], axis=1)
        past = k_past.shape[1]
    o = stick_breaking_attention(q, keys, vals, W['sb_bias'], past).reshape(b, T, D_ATT).astype(h.dtype)
    mix = jnp.concatenate([y_ssd, o], axis=-1) @ W['w_out']
    h = h + rms_norm(mix, W['mix_post_g'])
    h = h + 0.5 * rms_norm(swiglu_ffn(rms_norm(h, W['ffn2_pre_g']), W['ffn2_w_gu'], W['ffn2_w_down']), W['ffn2_post_g'])
    gate = jax.nn.sigmoid(rms_norm(h, W['ple_norm_g']) @ W['w_ple_gate'])
    h = h + gate * (p.astype(h.dtype) @ W['w_ple_proj'])
    return h, new_conv, h_new, k, v


def setup_inputs(seed: int = 0) -> dict:
    key = jax.random.key(seed)
    keys = iter(jax.random.split(key, 48))
    f32 = jnp.float32
    n_pages = PAST_LEN // PAGE_SIZE
    n_used = DEC_BATCH * n_pages
    n_pool = n_used + n_used // 4

    def nrm(shape, scale):
        return jax.random.normal(next(keys), shape, f32) * scale

    def gain(shape):
        return 1.0 + 0.05 * jax.random.normal(next(keys), shape, f32)

    out = {}
    out['x_prompt'] = nrm((BATCH, SEQ, D_MODEL), 1.0)
    out['x_sample'] = nrm((DEC_BATCH, DEC_SEQ, D_MODEL), 1.0)
    out['p_prompt'] = nrm((DEPTH, BATCH, SEQ, PLE_DIM), 1.0)
    out['p_sample'] = nrm((DEPTH, DEC_BATCH, DEC_SEQ, PLE_DIM), 1.0)
    out['cache_k'] = nrm((DEPTH, n_pool, PAGE_SIZE, H_ATT, HEAD_DIM), 1.0)
    out['cache_v'] = nrm((DEPTH, n_pool, PAGE_SIZE, H_ATT, HEAD_DIM), 1.0)
    out['page_table'] = jax.random.permutation(next(keys), n_pool)[:n_used].reshape(DEC_BATCH, n_pages).astype(jnp.int32)
    out['state_conv'] = nrm((DEPTH, DEC_BATCH, CONV_W - 1, CONV_DIM), 1.0)
    out['state_ssm'] = nrm((DEPTH, DEC_BATCH, H_SSM, HEAD_DIM, D_STATE), 0.1)
    out['ffn1_pre_g'] = gain((DEPTH, D_MODEL))
    out['ffn1_w_gu'] = nrm((DEPTH, D_MODEL, 2 * D_FF), D_MODEL ** -0.5)
    out['ffn1_w_down'] = nrm((DEPTH, D_FF, D_MODEL), D_FF ** -0.5)
    out['ffn1_post_g'] = gain((DEPTH, D_MODEL))
    out['mix_pre_g'] = gain((DEPTH, D_MODEL))
    out['w_in'] = nrm((DEPTH, D_MODEL, N_IN), D_MODEL ** -0.5)
    out['conv_w'] = jax.random.uniform(next(keys), (DEPTH, CONV_W, CONV_DIM), f32, -0.5, 0.5)
    out['conv_b'] = nrm((DEPTH, CONV_DIM), 0.02)
    dt0 = jnp.exp(jax.random.uniform(next(keys), (DEPTH, H_SSM), f32, math.log(1e-3), math.log(1e-1)))
    out['dt_bias'] = dt0 + jnp.log(-jnp.expm1(-dt0))
    out['a_log'] = jnp.log(jax.random.uniform(next(keys), (DEPTH, H_SSM), f32, 1.0, 16.0))
    out['d_skip'] = gain((DEPTH, H_SSM))
    out['ssm_norm_g'] = gain((DEPTH, D_SSM))
    out['sb_bias'] = SB_BIAS_INIT + nrm((DEPTH, H_ATT), 0.3)
    out['w_out'] = nrm((DEPTH, D_MIX, D_MODEL), D_MIX ** -0.5)
    out['mix_post_g'] = gain((DEPTH, D_MODEL))
    out['ffn2_pre_g'] = gain((DEPTH, D_MODEL))
    out['ffn2_w_gu'] = nrm((DEPTH, D_MODEL, 2 * D_FF), D_MODEL ** -0.5)
    out['ffn2_w_down'] = nrm((DEPTH, D_FF, D_MODEL), D_FF ** -0.5)
    out['ffn2_post_g'] = gain((DEPTH, D_MODEL))
    out['ple_norm_g'] = gain((DEPTH, D_MODEL))
    out['w_ple_gate'] = nrm((DEPTH, D_MODEL, D_MODEL), D_MODEL ** -0.5)
    out['w_ple_proj'] = nrm((DEPTH, PLE_DIM, D_MODEL), PLE_DIM ** -0.5)
    return out


def reference(x_prompt, x_sample, p_prompt, p_sample, cache_k, cache_v, page_table, state_conv, state_ssm,
              ffn1_pre_g, ffn1_w_gu, ffn1_w_down, ffn1_post_g, mix_pre_g, w_in, conv_w, conv_b, dt_bias,
              a_log, d_skip, ssm_norm_g, sb_bias, w_out, mix_post_g, ffn2_pre_g, ffn2_w_gu, ffn2_w_down,
              ffn2_post_g, ple_norm_g, w_ple_gate, w_ple_proj):
    bp = x_prompt.shape[0]
    bs = x_sample.shape[0]
    n_pages = page_table.shape[1]
    past_len = n_pages * PAGE_SIZE
    conv0 = jnp.zeros((bp, CONV_W - 1, CONV_DIM), x_prompt.dtype)
    ssm0 = jnp.zeros((bp, H_SSM, HEAD_DIM, D_STATE), jnp.float32)
    hp, hs = x_prompt, x_sample
    kp_l, vp_l, cp_l, sp_l = [], [], [], []
    ks_l, vs_l, cs_l, ss_l = [], [], [], []
    for i in range(DEPTH):
        W = dict(ffn1_pre_g=ffn1_pre_g[i], ffn1_w_gu=ffn1_w_gu[i], ffn1_w_down=ffn1_w_down[i],
                 ffn1_post_g=ffn1_post_g[i], mix_pre_g=mix_pre_g[i], w_in=w_in[i], conv_w=conv_w[i],
                 conv_b=conv_b[i], dt_bias=dt_bias[i], a_log=a_log[i], d_skip=d_skip[i],
                 ssm_norm_g=ssm_norm_g[i], sb_bias=sb_bias[i], w_out=w_out[i], mix_post_g=mix_post_g[i],
                 ffn2_pre_g=ffn2_pre_g[i], ffn2_w_gu=ffn2_w_gu[i], ffn2_w_down=ffn2_w_down[i],
                 ffn2_post_g=ffn2_post_g[i], ple_norm_g=ple_norm_g[i], w_ple_gate=w_ple_gate[i],
                 w_ple_proj=w_ple_proj[i])
        hp, c_p, s_p, k_p, v_p = decoder_layer(hp, p_prompt[i], conv0, ssm0, None, None, W)
        k_past = cache_k[i][page_table].reshape(bs, past_len, H_ATT, HEAD_DIM)
        v_past = cache_v[i][page_table].reshape(bs, past_len, H_ATT, HEAD_DIM)
        hs, c_s, s_s, k_s, v_s = decoder_layer(hs, p_sample[i], state_conv[i], state_ssm[i], k_past, v_past, W)
        kp_l.append(k_p); vp_l.append(v_p); cp_l.append(c_p); sp_l.append(s_p)
        ks_l.append(k_s); vs_l.append(v_s); cs_l.append(c_s); ss_l.append(s_s)
    k_prompt = jnp.stack(kp_l)
    v_prompt = jnp.stack(vp_l)
    conv_prompt = jnp.stack(cp_l)
    ssm_prompt = jnp.stack(sp_l)
    k_sample = jnp.stack(ks_l)
    v_sample = jnp.stack(vs_l)
    conv_sample = jnp.stack(cs_l)
    ssm_sample = jnp.stack(ss_l)
    return (hp, hs, k_prompt, v_prompt, conv_prompt, ssm_prompt, k_sample, v_sample, conv_sample, ssm_sample)
```

```python
import functools
import math

import jax
import jax.numpy as jnp
from jax import lax
from jax.experimental import pallas as pl
from jax.experimental.pallas import tpu as pltpu

BF16 = jnp.bfloat16
F32 = jnp.float32

HEAD_DIM = 64
D_STATE = 128
N_BC_GROUPS = 2
CONV_W = 4
SSD_CHUNK = 128
PAGE_SIZE = 128
RMS_EPS = 1e-6

LANES = 128
SUBLANES = 8
VMEM_LIMIT_BYTES = 56 * 1024 * 1024

ROW_TILE = 512
COL_TILE = 512
ATT_TILE = 256
SAMPLE_ROWS = 16


def _params(*semantics):
    return pltpu.CompilerParams(dimension_semantics=semantics,
                                vmem_limit_bytes=VMEM_LIMIT_BYTES)


def _rms(x, g):
    ms = jnp.mean(x * x, axis=-1, keepdims=True)
    return x * lax.rsqrt(ms + RMS_EPS) * g


def _silu(x):
    return x * jax.nn.sigmoid(x)


def _softplus(x):
    return jnp.maximum(x, 0.0) + jnp.log1p(jnp.exp(-jnp.abs(x)))


def _dot(a, b):
    return jnp.dot(a, b, preferred_element_type=F32)


def _dot_nt(a, b):
    return lax.dot_general(a, b, (((1,), (1,)), ((), ())), preferred_element_type=F32)


def _dot_tn(a, b):
    return lax.dot_general(a, b, (((0,), (0,)), ((), ())), preferred_element_type=F32)


def _split2(x):
    hi = x.astype(BF16)
    lo = (x - hi.astype(F32)).astype(BF16)
    return hi, lo


def _split3(x):
    hi = x.astype(BF16)
    r = x - hi.astype(F32)
    mid = r.astype(BF16)
    lo = (r - mid.astype(F32)).astype(BF16)
    return hi, mid, lo


def _ffn_body(h_ref, pre_ref, wg_ref, wu_ref, wd_ref, post_ref, o_ref, u_ref, acc_ref, *, nf):
    j = pl.program_id(1)

    @pl.when(j == 0)
    def _():
        u_ref[...] = _rms(h_ref[...], pre_ref[...]).astype(BF16)

    u = u_ref[...]
    g = _dot(u, wg_ref[...])
    up = _dot(u, wu_ref[...])
    a = (_silu(g) * up).astype(BF16)
    d = _dot(a, wd_ref[...])

    @pl.when(j == 0)
    def _():
        acc_ref[...] = d

    @pl.when(j > 0)
    def _():
        acc_ref[...] += d

    @pl.when(j == nf - 1)
    def _():
        o_ref[...] = h_ref[...] + 0.5 * _rms(acc_ref[...], post_ref[...])


def _ffn(h, pre_g, w_gu, w_down, post_g, layer, tm):
    m, d = h.shape
    f = w_down.shape[1]
    tf = COL_TILE
    nf = f // tf
    return pl.pallas_call(
        functools.partial(_ffn_body, nf=nf),
        grid=(m // tm, nf),
        in_specs=[
            pl.BlockSpec((tm, d), lambda i, j: (i, 0)),
            pl.BlockSpec((None, 1, d), lambda i, j: (layer, 0, 0)),
            pl.BlockSpec((None, d, tf), lambda i, j: (layer, 0, j)),
            pl.BlockSpec((None, d, tf), lambda i, j: (layer, 0, j + nf)),
            pl.BlockSpec((None, tf, d), lambda i, j: (layer, j, 0)),
            pl.BlockSpec((None, 1, d), lambda i, j: (layer, 0, 0)),
        ],
        out_specs=pl.BlockSpec((tm, d), lambda i, j: (i, 0)),
        out_shape=jax.ShapeDtypeStruct((m, d), F32),
        scratch_shapes=[pltpu.VMEM((tm, d), BF16), pltpu.VMEM((tm, d), F32)],
        compiler_params=_params("parallel", "arbitrary"),
        name="ffn",
    )(h, pre_g, w_gu, w_gu, w_down, post_g)


_PROJ_SEGMENTS = ((0, 2), (2, 2), (4, 2), (6, 2), (8, 2), (10, 1))


def _proj_body(h_ref, g_ref, w_ref, wdt_ref, q_ref, k_ref, v_ref, z_ref, x_ref, bc_ref, dt_ref, u_ref):
    j = pl.program_id(1)

    @pl.when(j == 0)
    def _():
        u = _rms(h_ref[...], g_ref[...]).astype(BF16)
        u_ref[...] = u
        dt_ref[...] = _dot(u, wdt_ref[...])

    r = _dot(u_ref[...], w_ref[...])
    for ref, (start, count) in zip((q_ref, k_ref, v_ref, z_ref, x_ref, bc_ref), _PROJ_SEGMENTS):
        @pl.when((j >= start) & (j < start + count))
        def _(ref=ref):
            ref[...] = r


def _proj(h, g, w_main, w_dt, layer, tm):
    m, d = h.shape
    tn = COL_TILE
    n_tiles = w_main.shape[2] // tn

    def seg_spec(start, count):
        return pl.BlockSpec((tm, tn), lambda i, j: (i, jnp.clip(j - start, 0, count - 1)))

    out_specs = [seg_spec(s, c) for s, c in _PROJ_SEGMENTS]
    out_specs.append(pl.BlockSpec((tm, LANES), lambda i, j: (i, 0)))
    out_shape = [jax.ShapeDtypeStruct((m, c * tn), F32) for _, c in _PROJ_SEGMENTS]
    out_shape.append(jax.ShapeDtypeStruct((m, LANES), F32))
    return pl.pallas_call(
        _proj_body,
        grid=(m // tm, n_tiles),
        in_specs=[
            pl.BlockSpec((tm, d), lambda i, j: (i, 0)),
            pl.BlockSpec((None, 1, d), lambda i, j: (layer, 0, 0)),
            pl.BlockSpec((None, d, tn), lambda i, j: (layer, 0, j)),
            pl.BlockSpec((None, d, LANES), lambda i, j: (layer, 0, 0)),
        ],
        out_specs=out_specs,
        out_shape=out_shape,
        scratch_shapes=[pltpu.VMEM((tm, d), BF16)],
        compiler_params=_params("parallel", "arbitrary"),
        name="proj",
    )(h, g, w_main, w_dt)


def _outproj_body(h_ref, y_ref, a_ref, wy_ref, wa_ref, g_ref, o_ref, mix_ref, *, nn, tn):
    j = pl.program_id(1)
    mix_ref[j] = (_dot(y_ref[...].astype(BF16), wy_ref[...])
                  + _dot(a_ref[...].astype(BF16), wa_ref[...]))

    @pl.when(j == nn - 1)
    def _():
        ss = jnp.zeros((h_ref.shape[0], 1), F32)
        for t in range(nn):
            mt = mix_ref[t]
            ss = ss + jnp.sum(mt * mt, axis=-1, keepdims=True)
        rs = lax.rsqrt(ss / (nn * tn) + RMS_EPS)
        for t in range(nn):
            cols = slice(t * tn, (t + 1) * tn)
            o_ref[:, cols] = h_ref[:, cols] + mix_ref[t] * rs * g_ref[:, cols]


def _outproj(h, y, a, w_out, g, layer, tm):
    m, d = h.shape
    dy = y.shape[1]
    da = a.shape[1]
    tn = COL_TILE
    nn = d // tn
    return pl.pallas_call(
        functools.partial(_outproj_body, nn=nn, tn=tn),
        grid=(m // tm, nn),
        in_specs=[
            pl.BlockSpec((tm, d), lambda i, j: (i, 0)),
            pl.BlockSpec((tm, dy), lambda i, j: (i, 0)),
            pl.BlockSpec((tm, da), lambda i, j: (i, 0)),
            pl.BlockSpec((None, dy, tn), lambda i, j: (layer, 0, j)),
            pl.BlockSpec((None, da, tn), lambda i, j: (layer, dy // da, j)),
            pl.BlockSpec((None, 1, d), lambda i, j: (layer, 0, 0)),
        ],
        out_specs=pl.BlockSpec((tm, d), lambda i, j: (i, 0)),
        out_shape=jax.ShapeDtypeStruct((m, d), F32),
        scratch_shapes=[pltpu.VMEM((nn, tm, tn), F32)],
        compiler_params=_params("parallel", "arbitrary"),
        name="outproj",
    )(h, y, a, w_out, w_out, g)


def _ple_body(h_ref, hcol_ref, p_ref, g_ref, wg_ref, wp_ref, o_ref, u_ref):
    j = pl.program_id(1)

    @pl.when(j == 0)
    def _():
        u_ref[...] = _rms(h_ref[...], g_ref[...]).astype(BF16)

    gate = jax.nn.sigmoid(_dot(u_ref[...], wg_ref[...]))
    o_ref[...] = hcol_ref[...] + gate * _dot(p_ref[...].astype(BF16), wp_ref[...])


def _ple(h, p, g, w_gate, w_proj, layer, tm):
    m, d = h.shape
    dp = p.shape[2]
    tn = COL_TILE
    return pl.pallas_call(
        _ple_body,
        grid=(m // tm, d // tn),
        in_specs=[
            pl.BlockSpec((tm, d), lambda i, j: (i, 0)),
            pl.BlockSpec((tm, tn), lambda i, j: (i, j)),
            pl.BlockSpec((None, tm, dp), lambda i, j: (layer, i, 0)),
            pl.BlockSpec((None, 1, d), lambda i, j: (layer, 0, 0)),
            pl.BlockSpec((None, d, tn), lambda i, j: (layer, 0, j)),
            pl.BlockSpec((None, dp, tn), lambda i, j: (layer, 0, j)),
        ],
        out_specs=pl.BlockSpec((tm, tn), lambda i, j: (i, j)),
        out_shape=jax.ShapeDtypeStruct((m, d), F32),
        scratch_shapes=[pltpu.VMEM((tm, d), BF16)],
        compiler_params=_params("parallel", "arbitrary"),
        name="ple",
    )(h, h, p, g, w_gate, w_proj)


def _ssd_body(x_ref, bc_ref, dt_ref, z_ref, cwx_ref, cbx_ref, cwbc_ref, cbbc_ref,
              dtb_ref, alog_ref, dskip_ref, ng_ref,
              y_ref, hfin_ref, tailx_ref, tailbc_ref,
              xpad_ref, bcpad_ref, state_ref, yacc_ref, *, nc, n_heads):
    c = pl.program_id(1)
    L = SSD_CHUNK
    P = HEAD_DIM
    N = D_STATE
    heads_per_group = n_heads // N_BC_GROUPS
    pad = SUBLANES

    @pl.when(c == 0)
    def _():
        xpad_ref[0:pad, :] = jnp.zeros((pad, xpad_ref.shape[1]), F32)
        bcpad_ref[0:pad, :] = jnp.zeros((pad, bcpad_ref.shape[1]), F32)
        state_ref[...] = jnp.zeros(state_ref.shape, F32)

    xpad_ref[pad:pad + L, :] = x_ref[...]
    bcpad_ref[pad:pad + L, :] = bc_ref[...]

    def conv(pad_ref, w_ref, b_ref):
        out = b_ref[...]
        for j in range(CONV_W):
            out = out + pad_ref[pl.ds(pad - (CONV_W - 1) + j, L), :] * w_ref[j:j + 1, :]
        return _silu(out)

    xs = conv(xpad_ref, cwx_ref, cbx_ref)
    bcs = conv(bcpad_ref, cwbc_ref, cbbc_ref)

    tail_x = xpad_ref[L:L + pad, :]
    tail_bc = bcpad_ref[L:L + pad, :]
    xpad_ref[0:pad, :] = tail_x
    bcpad_ref[0:pad, :] = tail_bc

    dtv = _softplus(dt_ref[...] + dtb_ref[...])
    a_neg = -jnp.exp(alog_ref[...])
    da = dtv * a_neg

    row = lax.broadcasted_iota(jnp.int32, (L, L), 0)
    col = lax.broadcasted_iota(jnp.int32, (L, L), 1)
    causal = row >= col
    tril = jnp.where(causal, 1.0, 0.0).astype(BF16)
    cum = sum(_dot(tril, part) for part in _split3(da))
    cum_t = cum.T
    cum_last = cum[L - 1:L, :]
    ecum = jnp.exp(cum)
    to_end = jnp.exp(cum_last - cum)
    chunk_decay = jnp.exp(cum_last)

    gn = N_BC_GROUPS * N
    b_groups = [bcs[:, g * N:(g + 1) * N].astype(BF16) for g in range(N_BC_GROUPS)]
    c_groups = [bcs[:, gn + g * N:gn + (g + 1) * N].astype(BF16) for g in range(N_BC_GROUPS)]
    cb = [_dot_nt(c_groups[g], b_groups[g]) for g in range(N_BC_GROUPS)]

    for h in range(n_heads):
        g = h // heads_per_group
        diff = cum[:, h:h + 1] - cum_t[h:h + 1, :]
        decay = jnp.exp(jnp.where(causal, diff, -jnp.inf))
        xdt = xs[:, h * P:(h + 1) * P] * dtv[:, h:h + 1]
        y_diag = _dot((cb[g] * decay).astype(BF16), xdt.astype(BF16))
        st = state_ref[h]
        y_off = _dot_nt(c_groups[g], st.astype(BF16)) * ecum[:, h:h + 1]
        yacc_ref[:, h * P:(h + 1) * P] = y_diag + y_off
        upd = _dot_tn((xdt * to_end[:, h:h + 1]).astype(BF16), b_groups[g])
        state_ref[h] = chunk_decay[:, h:h + 1] * st + upd

    y = yacc_ref[...] + xs * dskip_ref[...]
    y_ref[...] = _rms(y * _silu(z_ref[...]), ng_ref[...]).astype(y_ref.dtype)

    @pl.when(c == nc - 1)
    def _():
        hfin_ref[...] = state_ref[...]
        tailx_ref[...] = tail_x
        tailbc_ref[...] = tail_bc


def _ssd_prompt(x, bc, dt, z, cwx, cbx, cwbc, cbbc, dtb, alog, dskip, ng, batch):
    m, d_ssm = x.shape
    d_bc = bc.shape[1]
    L = SSD_CHUNK
    nc = m // batch // L
    n_heads = d_ssm // HEAD_DIM
    pad = SUBLANES
    row_spec = lambda w: pl.BlockSpec((L, w), lambda b, c: (b * nc + c, 0))
    const = lambda shape: pl.BlockSpec(shape, lambda b, c: (0,) * len(shape))
    return pl.pallas_call(
        functools.partial(_ssd_body, nc=nc, n_heads=n_heads),
        grid=(batch, nc),
        in_specs=[
            row_spec(d_ssm), row_spec(d_bc), row_spec(LANES), row_spec(d_ssm),
            const((CONV_W, d_ssm)), const((1, d_ssm)), const((CONV_W, d_bc)), const((1, d_bc)),
            const((1, LANES)), const((1, LANES)), const((1, d_ssm)), const((1, d_ssm)),
        ],
        out_specs=[
            row_spec(d_ssm),
            pl.BlockSpec((None, n_heads, HEAD_DIM, D_STATE), lambda b, c: (b, 0, 0, 0)),
            pl.BlockSpec((None, pad, d_ssm), lambda b, c: (b, 0, 0)),
            pl.BlockSpec((None, pad, d_bc), lambda b, c: (b, 0, 0)),
        ],
        out_shape=[
            jax.ShapeDtypeStruct((m, d_ssm), BF16),
            jax.ShapeDtypeStruct((batch, n_heads, HEAD_DIM, D_STATE), F32),
            jax.ShapeDtypeStruct((batch, pad, d_ssm), F32),
            jax.ShapeDtypeStruct((batch, pad, d_bc), F32),
        ],
        scratch_shapes=[
            pltpu.VMEM((L + pad, d_ssm), F32),
            pltpu.VMEM((L + pad, d_bc), F32),
            pltpu.VMEM((n_heads, HEAD_DIM, D_STATE), F32),
            pltpu.VMEM((L, d_ssm), F32),
        ],
        compiler_params=_params("parallel", "arbitrary"),
        name="ssd_prompt",
    )(x, bc, dt, z, cwx, cbx, cwbc, cbbc, dtb, alog, dskip, ng)


def _expand_heads(v, head_of_lane, n_heads):
    out = jnp.zeros(head_of_lane.shape, F32)
    for h in range(n_heads):
        out = jnp.where(head_of_lane == h, v[:, h:h + 1], out)
    return out


def _rows_to_tile(rows, width):
    rid = lax.broadcasted_iota(jnp.int32, (SUBLANES, width), 0)
    out = jnp.zeros((SUBLANES, width), F32)
    for r, v in enumerate(rows):
        out = jnp.where(rid == r, v, out)
    return out


def _ssd_step_body(x_ref, bc_ref, dt_ref, z_ref, conv_ref, ssm_ref, cw_ref, cb_ref,
                   dtb_ref, alog_ref, dskip_ref, ng_ref,
                   y_ref, conv_out_ref, ssm_out_ref, *, n_heads):
    b = pl.program_id(0)
    N = D_STATE
    d_ssm = x_ref.shape[1]
    half = d_ssm // N_BC_GROUPS

    new = jnp.concatenate([x_ref[pl.ds(b, 1), :], bc_ref[pl.ds(b, 1), :]], axis=1)
    prev = conv_ref[...]
    out = cb_ref[...]
    for j in range(CONV_W - 1):
        out = out + prev[j:j + 1, :] * cw_ref[j:j + 1, :]
    out = out + new * cw_ref[CONV_W - 1:CONV_W, :]
    act = _silu(out)
    for j in range(CONV_W - 2):
        conv_out_ref[j:j + 1, :] = prev[j + 1:j + 2, :]
    conv_out_ref[CONV_W - 2:CONV_W - 1, :] = new

    xs = act[:, :d_ssm]
    gn = N_BC_GROUPS * N
    b_rows = [act[:, d_ssm + g * N:d_ssm + (g + 1) * N] for g in range(N_BC_GROUPS)]
    c_rows = [act[:, d_ssm + gn + g * N:d_ssm + gn + (g + 1) * N] for g in range(N_BC_GROUPS)]

    head_of_lane = lax.broadcasted_iota(jnp.int32, (1, d_ssm), 1) // HEAD_DIM
    group0 = lax.broadcasted_iota(jnp.int32, (1, d_ssm), 1) < half
    dtv = _softplus(dt_ref[pl.ds(b, 1), :] + dtb_ref[...])
    decay = jnp.exp(dtv * -jnp.exp(alog_ref[...]))
    dt_x = _expand_heads(dtv, head_of_lane, n_heads)
    decay_x = _expand_heads(decay, head_of_lane, n_heads)
    xdt = xs * dt_x

    cbs = [jnp.sum(c_rows[g] * b_rows[g], axis=1, keepdims=True) for g in range(N_BC_GROUPS)]
    y_diag = jnp.where(group0, cbs[0], cbs[1]) * xdt

    st = ssm_ref[...]
    st_hi = st.astype(BF16)
    st_lo = (st - st_hi.astype(F32)).astype(BF16)
    c_parts = [_split2(c_rows[g]) for g in range(N_BC_GROUPS)]
    c_tile = _rows_to_tile([c_parts[0][0].astype(F32), c_parts[0][1].astype(F32),
                            c_parts[1][0].astype(F32), c_parts[1][1].astype(F32)], N).astype(BF16)
    r_hi = _dot_nt(c_tile, st_hi)
    r_lo = _dot_nt(c_tile, st_lo)
    off0 = r_hi[0:1, :] + r_hi[1:2, :] + r_lo[0:1, :] + r_lo[1:2, :]
    off1 = r_hi[2:3, :] + r_hi[3:4, :] + r_lo[2:3, :] + r_lo[3:4, :]
    y_off = jnp.where(group0, off0, off1) * decay_x

    y = y_diag + y_off + xs * dskip_ref[...]
    y_ref[...] = _rms(y * _silu(z_ref[pl.ds(b, 1), :]), ng_ref[...])

    parts = [p.astype(F32) for p in _split3(xdt)] + [p.astype(F32) for p in _split3(decay_x)]
    tile = _rows_to_tile(parts, d_ssm).astype(BF16)
    rid = lax.broadcasted_iota(jnp.int32, (SUBLANES, N), 0)
    pick_x = jnp.where(rid < 3, 1.0, 0.0).astype(BF16)
    pick_d = jnp.where((rid >= 3) & (rid < 6), 1.0, 0.0).astype(BF16)
    x_col = _dot_tn(tile, pick_x)
    d_col = _dot_tn(tile, pick_d)
    ssm_out_ref[0:half, :] = d_col[0:half] * st[0:half] + x_col[0:half] * b_rows[0]
    ssm_out_ref[half:, :] = d_col[half:] * st[half:] + x_col[half:] * b_rows[1]


def _ssd_sample(x, bc, dt, z, state_conv, state_ssm, conv_w, conv_b, dtb, alog, dskip, ng, layer, batch):
    d_ssm = x.shape[1]
    n_heads = d_ssm // HEAD_DIM
    conv_dim = state_conv.shape[3]
    rows = n_heads * HEAD_DIM
    whole = lambda a: pl.BlockSpec(a.shape, lambda b: (0,) * a.ndim)
    return pl.pallas_call(
        functools.partial(_ssd_step_body, n_heads=n_heads),
        grid=(batch,),
        in_specs=[
            whole(x), whole(bc), whole(dt), whole(z),
            pl.BlockSpec((None, None, CONV_W - 1, conv_dim), lambda b: (layer, b, 0, 0)),
            pl.BlockSpec((None, None, rows, D_STATE), lambda b: (layer, b, 0, 0)),
            whole(conv_w), whole(conv_b), whole(dtb), whole(alog), whole(dskip), whole(ng),
        ],
        out_specs=[
            pl.BlockSpec((None, 1, d_ssm), lambda b: (b, 0, 0)),
            pl.BlockSpec((None, CONV_W - 1, conv_dim), lambda b: (b, 0, 0)),
            pl.BlockSpec((None, rows, D_STATE), lambda b: (b, 0, 0)),
        ],
        out_shape=[
            jax.ShapeDtypeStruct((batch, 1, d_ssm), F32),
            jax.ShapeDtypeStruct((batch, CONV_W - 1, conv_dim), F32),
            jax.ShapeDtypeStruct((batch, rows, D_STATE), F32),
        ],
        compiler_params=_params("arbitrary"),
        name="ssd_sample",
    )(x, bc, dt, z, state_conv, state_ssm, conv_w, conv_b, dtb, alog, dskip, ng)


def _sb_tile(qh, k_bf, v_bf, bias, suffix, carry, mask):
    acc, run = carry
    z = _dot_nt(qh, k_bf) + bias
    log_keep = -(jnp.maximum(z, 0.0) + jnp.log(1.0 + jnp.exp(-jnp.abs(z))))
    if mask is not None:
        log_keep = jnp.where(mask, log_keep, 0.0)
    hi, lo = _split2(log_keep)
    csum = _dot(hi, suffix) + _dot(lo, suffix)
    a = jnp.exp(z + csum + run)
    if mask is not None:
        a = jnp.where(mask, a, 0.0)
    acc = acc + _dot(a.astype(BF16), v_bf)
    run = run + csum[:, 0:1]
    return acc, run


def _sb_prompt_body(bias_ref, q_ref, k_ref, v_ref, o_ref):
    hp = pl.program_id(1)
    i = pl.program_id(2)
    t = ATT_TILE
    lane = lax.broadcasted_iota(jnp.int32, (t, LANES), 1)
    first = lane < HEAD_DIM
    q = q_ref[...] * (1.0 / math.sqrt(HEAD_DIM))
    q_heads = (jnp.where(first, q, 0.0).astype(BF16), jnp.where(first, 0.0, q).astype(BF16))
    biases = (bias_ref[2 * hp], bias_ref[2 * hp + 1])

    row = lax.broadcasted_iota(jnp.int32, (t, t), 0)
    col = lax.broadcasted_iota(jnp.int32, (t, t), 1)
    suffix = jnp.where(row >= col, 1.0, 0.0).astype(BF16)
    strictly_causal = col < row

    def tile(j, carries, mask):
        start = pl.multiple_of(j * t, t)
        k_bf = k_ref[pl.ds(start, t), :].astype(BF16)
        v_bf = v_ref[pl.ds(start, t), :].astype(BF16)
        return tuple(_sb_tile(q_heads[h], k_bf, v_bf, biases[h], suffix, carries[h], mask)
                     for h in range(2))

    zero = (jnp.zeros((t, LANES), F32), jnp.zeros((t, 1), F32))
    carries = tile(i, (zero, zero), strictly_causal)
    carries = lax.fori_loop(0, i, lambda s, cs: tile(i - 1 - s, cs, None), carries)
    o_ref[...] = jnp.where(first, carries[0][0], carries[1][0]).astype(o_ref.dtype)


def _sb_prompt(q, k, v, bias, batch):
    m, d_att = q.shape
    seq = m // batch
    t = ATT_TILE
    nq = seq // t
    return pl.pallas_call(
        _sb_prompt_body,
        grid=(batch, d_att // LANES, nq),
        in_specs=[
            pl.BlockSpec(memory_space=pltpu.SMEM),
            pl.BlockSpec((t, LANES), lambda b, hp, i: (b * nq + i, hp)),
            pl.BlockSpec((seq, LANES), lambda b, hp, i: (b, hp)),
            pl.BlockSpec((seq, LANES), lambda b, hp, i: (b, hp)),
        ],
        out_specs=pl.BlockSpec((t, LANES), lambda b, hp, i: (b * nq + i, hp)),
        out_shape=jax.ShapeDtypeStruct((m, d_att), BF16),
        compiler_params=_params("parallel", "parallel", "arbitrary"),
        name="sb_prompt",
    )(bias, q, k, v)


def _sb_sample_body(pt_ref, q_ref, bias_ref, seg_ref, segt_ref, k_ref, v_ref, o_ref, run_ref, acc_ref, *, n_pages):
    b = pl.program_id(0)
    j = pl.program_id(1)
    ps = PAGE_SIZE

    @pl.when(j == 0)
    def _():
        run_ref[...] = jnp.zeros(run_ref.shape, F32)
        acc_ref[...] = jnp.zeros(acc_ref.shape, F32)

    q = q_ref[pl.ds(b, 1), :] * (1.0 / math.sqrt(HEAD_DIM))
    prod_hi, prod_lo = _split2(k_ref[...] * q)
    z = _dot(prod_hi, seg_ref[...]) + _dot(prod_lo, seg_ref[...]) + bias_ref[...]
    log_keep = -(jnp.maximum(z, 0.0) + jnp.log(1.0 + jnp.exp(-jnp.abs(z))))
    row = lax.broadcasted_iota(jnp.int32, (ps, ps), 0)
    col = lax.broadcasted_iota(jnp.int32, (ps, ps), 1)
    suffix_t = jnp.where(col >= row, 1.0, 0.0).astype(BF16)
    hi, lo = _split2(log_keep)
    csum = _dot(suffix_t, hi) + _dot(suffix_t, lo)
    a = jnp.exp(z + csum + run_ref[...])
    run_ref[...] += csum[0:1, :]
    weights = _dot(a.astype(BF16), segt_ref[...])
    contrib = weights * v_ref[...]
    acc_ref[...] += jnp.sum(contrib.reshape(ps // SUBLANES, SUBLANES, contrib.shape[1]), axis=0)

    @pl.when(j == n_pages - 1)
    def _():
        o_ref[...] = jnp.sum(acc_ref[...], axis=0, keepdims=True)


def _sb_sample(q, bias_lanes, seg, seg_t, cache_k, cache_v, page_table, layer, batch):
    d_att = q.shape[1]
    n_pages = page_table.shape[1]
    page_spec = pl.BlockSpec((None, None, PAGE_SIZE, d_att),
                             lambda b, j, pt: (layer, pt[b, n_pages - 1 - j], 0, 0))
    whole = lambda a: pl.BlockSpec(a.shape, lambda b, j, pt: (0,) * a.ndim)
    return pl.pallas_call(
        functools.partial(_sb_sample_body, n_pages=n_pages),
        grid_spec=pltpu.PrefetchScalarGridSpec(
            num_scalar_prefetch=1,
            grid=(batch, n_pages),
            in_specs=[whole(q), whole(bias_lanes), whole(seg), whole(seg_t), page_spec, page_spec],
            out_specs=pl.BlockSpec((None, 1, d_att), lambda b, j, pt: (b, 0, 0)),
            scratch_shapes=[pltpu.VMEM((1, LANES), F32), pltpu.VMEM((SUBLANES, d_att), F32)],
        ),
        out_shape=jax.ShapeDtypeStruct((batch, 1, d_att), F32),
        compiler_params=_params("arbitrary", "arbitrary"),
        name="sb_sample",
    )(page_table, q, bias_lanes, seg, seg_t, cache_k, cache_v)


def _pad_lanes(a):
    return jnp.pad(a, ((0, 0), (0, LANES - a.shape[1])))[:, None, :]


def kernel(x_prompt, x_sample, p_prompt, p_sample, cache_k, cache_v, page_table, state_conv, state_ssm, ffn1_pre_g, ffn1_w_gu, ffn1_w_down, ffn1_post_g, mix_pre_g, w_in, conv_w, conv_b, dt_bias, a_log, d_skip, ssm_norm_g, sb_bias, w_out, mix_post_g, ffn2_pre_g, ffn2_w_gu, ffn2_w_down, ffn2_post_g, ple_norm_g, w_ple_gate, w_ple_proj):
    depth = w_in.shape[0]
    bp, seq, d_model = x_prompt.shape
    bs = x_sample.shape[0]
    d_ssm = ssm_norm_g.shape[1]
    n_ssm_heads = d_skip.shape[1]
    n_att_heads = sb_bias.shape[1]
    d_att = n_att_heads * HEAD_DIM
    d_bc = 2 * N_BC_GROUPS * D_STATE
    conv_dim = d_ssm + d_bc
    n_pool = cache_k.shape[1]

    off_xbc = d_ssm
    off_dt = off_xbc + conv_dim
    off_q = off_dt + n_ssm_heads
    off_k = off_q + d_att
    off_v = off_k + d_att

    w_main = jnp.concatenate(
        [w_in[:, :, off_q:], w_in[:, :, :off_xbc], w_in[:, :, off_xbc:off_dt]], axis=2).astype(BF16)
    w_dt = jnp.pad(w_in[:, :, off_dt:off_q], ((0, 0), (0, 0), (0, LANES - n_ssm_heads))).astype(BF16)
    w_gu1, w_dn1 = ffn1_w_gu.astype(BF16), ffn1_w_down.astype(BF16)
    w_gu2, w_dn2 = ffn2_w_gu.astype(BF16), ffn2_w_down.astype(BF16)
    w_o = w_out.astype(BF16)
    w_pg, w_pp = w_ple_gate.astype(BF16), w_ple_proj.astype(BF16)

    row3 = lambda g: g[:, None, :]
    g_f1a, g_f1b = row3(ffn1_pre_g), row3(ffn1_post_g)
    g_f2a, g_f2b = row3(ffn2_pre_g), row3(ffn2_post_g)
    g_mixa, g_mixb = row3(mix_pre_g), row3(mix_post_g)
    g_ple = row3(ple_norm_g)
    dtb, alog = _pad_lanes(dt_bias), _pad_lanes(a_log)
    dskip = jnp.repeat(d_skip, HEAD_DIM, axis=1)[:, None, :]
    ng = row3(ssm_norm_g)
    bias_lanes = _pad_lanes(sb_bias)

    lane_head = jnp.arange(d_att, dtype=jnp.int32) // HEAD_DIM
    seg = (lane_head[:, None] == jnp.arange(LANES, dtype=jnp.int32)[None, :]).astype(BF16)
    seg_t = seg.T

    cache_k4 = cache_k.reshape(depth, n_pool, PAGE_SIZE, d_att)
    cache_v4 = cache_v.reshape(depth, n_pool, PAGE_SIZE, d_att)
    ssm_rows = state_ssm.reshape(depth, bs, n_ssm_heads * HEAD_DIM, D_STATE)

    mp = bp * seq
    hp = x_prompt.reshape(mp, d_model)
    hs = jnp.pad(x_sample.reshape(bs, d_model), ((0, SAMPLE_ROWS - bs), (0, 0)))
    pp = p_prompt.reshape(depth, mp, p_prompt.shape[3])
    psm = jnp.pad(p_sample.reshape(depth, bs, p_sample.shape[3]), ((0, 0), (0, SAMPLE_ROWS - bs), (0, 0)))

    outs = {name: [] for name in ("kp", "vp", "cp", "sp", "ks", "vs", "cs", "ss")}
    for i in range(depth):
        hp = _ffn(hp, g_f1a, w_gu1, w_dn1, g_f1b, i, ROW_TILE)
        q, k, v, z, x, bc, dt = _proj(hp, g_mixa, w_main, w_dt, i, ROW_TILE)
        y, hfin, tail_x, tail_bc = _ssd_prompt(
            x, bc, dt, z,
            conv_w[i][:, :d_ssm], conv_b[i][None, :d_ssm], conv_w[i][:, d_ssm:], conv_b[i][None, d_ssm:],
            dtb[i], alog[i], dskip[i], ng[i], bp)
        o = _sb_prompt(q, k, v, sb_bias[i], bp)
        hp = _outproj(hp, y, o, w_o, g_mixb, i, ROW_TILE)
        hp = _ffn(hp, g_f2a, w_gu2, w_dn2, g_f2b, i, ROW_TILE)
        hp = _ple(hp, pp, g_ple, w_pg, w_pp, i, ROW_TILE)
        keep = SUBLANES - (CONV_W - 1)
        outs["kp"].append(k.reshape(bp, seq, n_att_heads, HEAD_DIM))
        outs["vp"].append(v.reshape(bp, seq, n_att_heads, HEAD_DIM))
        outs["cp"].append(jnp.concatenate([tail_x[:, keep:], tail_bc[:, keep:]], axis=2))
        outs["sp"].append(hfin)

        hs = _ffn(hs, g_f1a, w_gu1, w_dn1, g_f1b, i, SAMPLE_ROWS)
        q, k, v, z, x, bc, dt = _proj(hs, g_mixa, w_main, w_dt, i, SAMPLE_ROWS)
        y, conv_new, ssm_new = _ssd_sample(
            x, bc, dt, z, state_conv, ssm_rows, conv_w[i], conv_b[i][None, :],
            dtb[i], alog[i], dskip[i], ng[i], i, bs)
        o = _sb_sample(q, bias_lanes[i], seg, seg_t, cache_k4, cache_v4, page_table, i, bs)
        pad_rows = lambda a: jnp.pad(a.reshape(bs, -1), ((0, SAMPLE_ROWS - bs), (0, 0)))
        hs = _outproj(hs, pad_rows(y), pad_rows(o), w_o, g_mixb, i, SAMPLE_ROWS)
        hs = _ffn(hs, g_f2a, w_gu2, w_dn2, g_f2b, i, SAMPLE_ROWS)
        hs = _ple(hs, psm, g_ple, w_pg, w_pp, i, SAMPLE_ROWS)
        outs["ks"].append(k[:bs].reshape(bs, 1, n_att_heads, HEAD_DIM))
        outs["vs"].append(v[:bs].reshape(bs, 1, n_att_heads, HEAD_DIM))
        outs["cs"].append(conv_new)
        outs["ss"].append(ssm_new.reshape(bs, n_ssm_heads, HEAD_DIM, D_STATE))

    stack = lambda name: jnp.stack(outs[name])
    return (hp.reshape(bp, seq, d_model), hs[:bs].reshape(bs, 1, d_model),
            stack("kp"), stack("vp"), stack("cp"), stack("sp"),
            stack("ks"), stack("vs"), stack("cs"), stack("ss"))
```

```python
import functools
import math

import jax
import jax.numpy as jnp
from jax import lax
from jax.experimental import pallas as pl
from jax.experimental.pallas import tpu as pltpu

BF16 = jnp.bfloat16
F32 = jnp.float32

HEAD_DIM = 64
D_STATE = 128
N_BC_GROUPS = 2
CONV_W = 4
SSD_CHUNK = 128
PAGE_SIZE = 128
RMS_EPS = 1e-6

LANES = 128
SUBLANES = 8
VMEM_LIMIT_BYTES = 56 * 1024 * 1024

ROW_TILE = 512
COL_TILE = 512
ATT_TILE = 256
SAMPLE_ROWS = 16
SAMPLE_PAGES_PER_STEP = 4


def _params(*semantics):
    return pltpu.CompilerParams(dimension_semantics=semantics,
                                vmem_limit_bytes=VMEM_LIMIT_BYTES)


def _rms(x, g):
    ms = jnp.mean(x * x, axis=-1, keepdims=True)
    return x * lax.rsqrt(ms + RMS_EPS) * g


def _silu(x):
    return x * jax.nn.sigmoid(x)


def _softplus(x):
    return jnp.maximum(x, 0.0) + jnp.log1p(jnp.exp(-jnp.abs(x)))


def _dot(a, b):
    return jnp.dot(a, b, preferred_element_type=F32)


def _dot_nt(a, b):
    return lax.dot_general(a, b, (((1,), (1,)), ((), ())), preferred_element_type=F32)


def _dot_tn(a, b):
    return lax.dot_general(a, b, (((0,), (0,)), ((), ())), preferred_element_type=F32)


def _split2(x):
    hi = x.astype(BF16)
    lo = (x - hi.astype(F32)).astype(BF16)
    return hi, lo


def _split3(x):
    hi = x.astype(BF16)
    r = x - hi.astype(F32)
    mid = r.astype(BF16)
    lo = (r - mid.astype(F32)).astype(BF16)
    return hi, mid, lo


def _ffn_body(h_ref, pre_ref, wg_ref, wu_ref, wd_ref, post_ref, o_ref, u_ref, acc_ref, *, nf):
    j = pl.program_id(1)

    @pl.when(j == 0)
    def _():
        u_ref[...] = _rms(h_ref[...], pre_ref[...]).astype(BF16)
        acc_ref[...] = jnp.zeros(acc_ref.shape, F32)

    u = u_ref[...]
    g = _dot(u, wg_ref[...])
    up = _dot(u, wu_ref[...])
    a = (_silu(g) * up).astype(BF16)
    acc_ref[...] += _dot(a, wd_ref[...])

    @pl.when(j == nf - 1)
    def _():
        o_ref[...] = h_ref[...] + 0.5 * _rms(acc_ref[...], post_ref[...])


def _ffn(h, pre_g, w_gu, w_down, post_g, layer, tm):
    m, d = h.shape
    f = w_down.shape[1]
    tf = COL_TILE
    nf = f // tf
    return pl.pallas_call(
        functools.partial(_ffn_body, nf=nf),
        grid=(m // tm, nf),
        in_specs=[
            pl.BlockSpec((tm, d), lambda i, j: (i, 0)),
            pl.BlockSpec((None, 1, d), lambda i, j: (layer, 0, 0)),
            pl.BlockSpec((None, d, tf), lambda i, j: (layer, 0, j)),
            pl.BlockSpec((None, d, tf), lambda i, j: (layer, 0, j + nf)),
            pl.BlockSpec((None, tf, d), lambda i, j: (layer, j, 0)),
            pl.BlockSpec((None, 1, d), lambda i, j: (layer, 0, 0)),
        ],
        out_specs=pl.BlockSpec((tm, d), lambda i, j: (i, 0)),
        out_shape=jax.ShapeDtypeStruct((m, d), F32),
        scratch_shapes=[pltpu.VMEM((tm, d), BF16), pltpu.VMEM((tm, d), F32)],
        compiler_params=_params("parallel", "arbitrary"),
        name="ffn",
    )(h, pre_g, w_gu, w_gu, w_down, post_g)


_PROJ_SEGMENTS = ((0, 2), (2, 2), (4, 2), (6, 2), (8, 2), (10, 1))
_Q_SCALE = 1.0 / math.sqrt(HEAD_DIM)


def _in_segment(j, seg):
    start, count = seg
    return (j >= start) & (j < start + count)


def _proj_common(h_ref, g_ref, w_ref, wdt_ref, dt_ref, u_ref, z_ref, x_ref, bc_ref):
    j = pl.program_id(1)

    @pl.when(j == 0)
    def _():
        u = _rms(h_ref[...], g_ref[...]).astype(BF16)
        u_ref[...] = u
        dt_ref[...] = _dot(u, wdt_ref[...])

    r = _dot(u_ref[...], w_ref[...])
    for ref, seg in zip((z_ref, x_ref, bc_ref), _PROJ_SEGMENTS[3:]):
        @pl.when(_in_segment(j, seg))
        def _(ref=ref):
            ref[...] = r
    return j, r


def _proj_rows_body(h_ref, g_ref, w_ref, wdt_ref, q_ref, k_ref, v_ref, z_ref, x_ref, bc_ref, dt_ref, u_ref):
    j, r = _proj_common(h_ref, g_ref, w_ref, wdt_ref, dt_ref, u_ref, z_ref, x_ref, bc_ref)

    @pl.when(_in_segment(j, _PROJ_SEGMENTS[0]))
    def _():
        q_ref[...] = r * _Q_SCALE

    for ref, seg in zip((k_ref, v_ref), _PROJ_SEGMENTS[1:3]):
        @pl.when(_in_segment(j, seg))
        def _(ref=ref):
            ref[...] = r


def _proj_cols_body(h_ref, g_ref, w_ref, wdt_ref, q_ref, kt_ref, vt_ref, ktb_ref, vtb_ref,
                    z_ref, x_ref, bc_ref, dt_ref, u_ref):
    j, r = _proj_common(h_ref, g_ref, w_ref, wdt_ref, dt_ref, u_ref, z_ref, x_ref, bc_ref)

    @pl.when(_in_segment(j, _PROJ_SEGMENTS[0]))
    def _():
        q_ref[...] = (r * _Q_SCALE).astype(q_ref.dtype)

    for ref, bref, seg in zip((kt_ref, vt_ref), (ktb_ref, vtb_ref), _PROJ_SEGMENTS[1:3]):
        @pl.when(_in_segment(j, seg))
        def _(ref=ref, bref=bref):
            rt = r.T
            ref[...] = rt
            bref[...] = rt.astype(BF16)


def _proj_in_specs(tm, d, tn, layer):
    return [
        pl.BlockSpec((tm, d), lambda i, j: (i, 0)),
        pl.BlockSpec((None, 1, d), lambda i, j: (layer, 0, 0)),
        pl.BlockSpec((None, d, tn), lambda i, j: (layer, 0, j)),
        pl.BlockSpec((None, d, LANES), lambda i, j: (layer, 0, 0)),
    ]


def _seg_col(j, seg):
    return jnp.clip(j - seg[0], 0, seg[1] - 1)


def _proj_rows(h, g, w_main, w_dt, layer):
    m, d = h.shape
    tn = COL_TILE
    out_specs = [pl.BlockSpec((m, tn), lambda i, j, seg=seg: (i, _seg_col(j, seg))) for seg in _PROJ_SEGMENTS]
    out_specs.append(pl.BlockSpec((m, LANES), lambda i, j: (i, 0)))
    out_shape = [jax.ShapeDtypeStruct((m, c * tn), F32) for _, c in _PROJ_SEGMENTS]
    out_shape.append(jax.ShapeDtypeStruct((m, LANES), F32))
    return pl.pallas_call(
        _proj_rows_body,
        grid=(1, w_main.shape[2] // tn),
        in_specs=_proj_in_specs(m, d, tn, layer),
        out_specs=out_specs,
        out_shape=out_shape,
        scratch_shapes=[pltpu.VMEM((m, d), BF16)],
        compiler_params=_params("parallel", "arbitrary"),
        name="proj_rows",
    )(h, g, w_main, w_dt)


def _proj_cols(h, g, w_main, w_dt, layer, batch):
    m, d = h.shape
    tn = COL_TILE
    tm = ROW_TILE
    seq = m // batch
    tiles_per_seq = seq // tm
    seg_q, seg_k, seg_v = _PROJ_SEGMENTS[:3]
    d_att = seg_k[1] * tn

    def row_spec(seg):
        return pl.BlockSpec((tm, tn), lambda i, j: (i, _seg_col(j, seg)))

    def t_spec(seg):
        return pl.BlockSpec((None, tn, tm),
                            lambda i, j: (i // tiles_per_seq, _seg_col(j, seg), i % tiles_per_seq))

    def slab_spec(seg):
        return pl.BlockSpec((None, None, tn, tm),
                            lambda i, j: (i // tiles_per_seq, i % tiles_per_seq, _seg_col(j, seg), 0))

    out_specs = [row_spec(seg_q), t_spec(seg_k), t_spec(seg_v), slab_spec(seg_k), slab_spec(seg_v)]
    out_specs += [row_spec(seg) for seg in _PROJ_SEGMENTS[3:]]
    out_specs.append(pl.BlockSpec((tm, LANES), lambda i, j: (i, 0)))
    out_shape = [
        jax.ShapeDtypeStruct((m, d_att), BF16),
        jax.ShapeDtypeStruct((batch, d_att, seq), F32),
        jax.ShapeDtypeStruct((batch, d_att, seq), F32),
        jax.ShapeDtypeStruct((batch, tiles_per_seq, d_att, tm), BF16),
        jax.ShapeDtypeStruct((batch, tiles_per_seq, d_att, tm), BF16),
    ]
    out_shape += [jax.ShapeDtypeStruct((m, c * tn), F32) for _, c in _PROJ_SEGMENTS[3:]]
    out_shape.append(jax.ShapeDtypeStruct((m, LANES), F32))
    return pl.pallas_call(
        _proj_cols_body,
        grid=(m // tm, w_main.shape[2] // tn),
        in_specs=_proj_in_specs(tm, d, tn, layer),
        out_specs=out_specs,
        out_shape=out_shape,
        scratch_shapes=[pltpu.VMEM((tm, d), BF16)],
        compiler_params=_params("parallel", "arbitrary"),
        name="proj_cols",
    )(h, g, w_main, w_dt)


def _outproj_body(h_ref, y_ref, a_ref, wy_ref, wa_ref, g_ref, o_ref, mix_ref, *, nn, tn):
    j = pl.program_id(1)
    mix_ref[j] = (_dot(y_ref[...].astype(BF16), wy_ref[...])
                  + _dot(a_ref[...].astype(BF16), wa_ref[...]))

    @pl.when(j == nn - 1)
    def _():
        ss = jnp.zeros((h_ref.shape[0], 1), F32)
        for t in range(nn):
            mt = mix_ref[t]
            ss = ss + jnp.sum(mt * mt, axis=-1, keepdims=True)
        rs = lax.rsqrt(ss / (nn * tn) + RMS_EPS)
        for t in range(nn):
            cols = slice(t * tn, (t + 1) * tn)
            o_ref[:, cols] = h_ref[:, cols] + mix_ref[t] * rs * g_ref[:, cols]


def _outproj(h, y, a, w_out, g, layer, tm):
    m, d = h.shape
    dy = y.shape[1]
    da = a.shape[1]
    tn = COL_TILE
    nn = d // tn
    return pl.pallas_call(
        functools.partial(_outproj_body, nn=nn, tn=tn),
        grid=(m // tm, nn),
        in_specs=[
            pl.BlockSpec((tm, d), lambda i, j: (i, 0)),
            pl.BlockSpec((tm, dy), lambda i, j: (i, 0)),
            pl.BlockSpec((tm, da), lambda i, j: (i, 0)),
            pl.BlockSpec((None, dy, tn), lambda i, j: (layer, 0, j)),
            pl.BlockSpec((None, da, tn), lambda i, j: (layer, dy // da, j)),
            pl.BlockSpec((None, 1, d), lambda i, j: (layer, 0, 0)),
        ],
        out_specs=pl.BlockSpec((tm, d), lambda i, j: (i, 0)),
        out_shape=jax.ShapeDtypeStruct((m, d), F32),
        scratch_shapes=[pltpu.VMEM((nn, tm, tn), F32)],
        compiler_params=_params("parallel", "arbitrary"),
        name="outproj",
    )(h, y, a, w_out, w_out, g)


def _ple_body(h_ref, hcol_ref, p_ref, g_ref, wg_ref, wp_ref, o_ref, u_ref):
    j = pl.program_id(1)

    @pl.when(j == 0)
    def _():
        u_ref[...] = _rms(h_ref[...], g_ref[...]).astype(BF16)

    gate = jax.nn.sigmoid(_dot(u_ref[...], wg_ref[...]))
    o_ref[...] = hcol_ref[...] + gate * _dot(p_ref[...].astype(BF16), wp_ref[...])


def _ple(h, p, g, w_gate, w_proj, layer, tm):
    m, d = h.shape
    dp = p.shape[2]
    tn = COL_TILE
    return pl.pallas_call(
        _ple_body,
        grid=(m // tm, d // tn),
        in_specs=[
            pl.BlockSpec((tm, d), lambda i, j: (i, 0)),
            pl.BlockSpec((tm, tn), lambda i, j: (i, j)),
            pl.BlockSpec((None, tm, dp), lambda i, j: (layer, i, 0)),
            pl.BlockSpec((None, 1, d), lambda i, j: (layer, 0, 0)),
            pl.BlockSpec((None, d, tn), lambda i, j: (layer, 0, j)),
            pl.BlockSpec((None, dp, tn), lambda i, j: (layer, 0, j)),
        ],
        out_specs=pl.BlockSpec((tm, tn), lambda i, j: (i, j)),
        out_shape=jax.ShapeDtypeStruct((m, d), F32),
        scratch_shapes=[pltpu.VMEM((tm, d), BF16)],
        compiler_params=_params("parallel", "arbitrary"),
        name="ple",
    )(h, h, p, g, w_gate, w_proj)


def _ssd_body(x_ref, bc_ref, dt_ref, z_ref, cwx_ref, cbx_ref, cwbc_ref, cbbc_ref,
              dtb_ref, alog_ref, dskip_ref, ng_ref, hexp_ref, hcol_ref,
              y_ref, hfin_ref, tailx_ref, tailbc_ref,
              xpad_ref, bcpad_ref, state_ref, *, nc, n_heads):
    c = pl.program_id(1)
    L = SSD_CHUNK
    P = HEAD_DIM
    N = D_STATE
    pairs_per_group = n_heads // N_BC_GROUPS // 2
    pad = SUBLANES

    @pl.when(c == 0)
    def _():
        xpad_ref[0:pad, :] = jnp.zeros((pad, xpad_ref.shape[1]), F32)
        bcpad_ref[0:pad, :] = jnp.zeros((pad, bcpad_ref.shape[1]), F32)
        state_ref[...] = jnp.zeros(state_ref.shape, F32)

    xpad_ref[pad:pad + L, :] = x_ref[...]
    bcpad_ref[pad:pad + L, :] = bc_ref[...]

    def conv(pad_ref, w_ref, b_ref):
        ext = pad_ref[...]
        out = b_ref[...]
        for j in range(CONV_W):
            back = CONV_W - 1 - j
            tap = ext if back == 0 else pltpu.roll(ext, back, axis=0)
            out = out + tap[pad:pad + L] * w_ref[j:j + 1, :]
        return _silu(out)

    xs = conv(xpad_ref, cwx_ref, cbx_ref)
    bcs = conv(bcpad_ref, cwbc_ref, cbbc_ref)

    tail_x = xpad_ref[L:L + pad, :]
    tail_bc = bcpad_ref[L:L + pad, :]
    xpad_ref[0:pad, :] = tail_x
    bcpad_ref[0:pad, :] = tail_bc

    def spread(v, ones_ref):
        return sum(_dot(part, ones_ref[...]) for part in _split3(v))

    dtv = _softplus(dt_ref[...] + dtb_ref[...])
    da = dtv * -jnp.exp(alog_ref[...])

    row = lax.broadcasted_iota(jnp.int32, (L, L), 0)
    col = lax.broadcasted_iota(jnp.int32, (L, L), 1)
    causal = row >= col
    tril = jnp.where(causal, 1.0, 0.0).astype(BF16)
    cum = sum(_dot(tril, part) for part in _split3(da))
    cum_t = cum.T
    cum_cols = spread(cum, hcol_ref)
    cum_x = spread(cum, hexp_ref)
    dt_x = spread(dtv, hexp_ref)
    cum_last_x = cum_x[L - 1:L, :]
    ecum_x = jnp.exp(cum_x)
    chunk_decay_x = jnp.exp(cum_last_x)
    xdt = xs * dt_x
    xdt_bf = xdt.astype(BF16)
    xdt_to_end = (xdt * jnp.exp(cum_last_x - cum_x)).astype(BF16)

    gn = N_BC_GROUPS * N
    b_groups = [bcs[:, g * N:(g + 1) * N].astype(BF16) for g in range(N_BC_GROUPS)]
    c_groups = [bcs[:, gn + g * N:gn + (g + 1) * N].astype(BF16) for g in range(N_BC_GROUPS)]
    cb = [_dot_nt(c_groups[g], b_groups[g]) for g in range(N_BC_GROUPS)]

    first = lax.broadcasted_iota(jnp.int32, (L, 2 * P), 1) < P
    first_rows = lax.broadcasted_iota(jnp.int32, (2 * P, N), 0) < P
    zero_bf = jnp.zeros((L, 2 * P), BF16)
    y_pairs = []
    for pr in range(n_heads // 2):
        g = pr // pairs_per_group
        lanes = slice(pr * 2 * P, (pr + 1) * 2 * P)
        weights = []
        for h in (2 * pr, 2 * pr + 1):
            diff = cum_cols[:, h * L:(h + 1) * L] - cum_t[h:h + 1, :]
            decay = jnp.exp(jnp.where(causal, diff, -jnp.inf))
            weights.append((cb[g] * decay).astype(BF16))
        xp = xdt_bf[:, lanes]
        y_diag = _dot(jnp.concatenate(weights, axis=1),
                      jnp.concatenate([jnp.where(first, xp, zero_bf), jnp.where(first, zero_bf, xp)], axis=0))
        st = state_ref[pr]
        y_off = _dot_nt(c_groups[g], st.astype(BF16)) * ecum_x[:, lanes]
        y_pairs.append(y_diag + y_off)
        upd = _dot_tn(xdt_to_end[:, lanes], b_groups[g])
        cd = chunk_decay_x[:, lanes]
        state_ref[pr] = jnp.where(first_rows, cd[:, 0:1], cd[:, P:P + 1]) * st + upd

    y = jnp.concatenate(y_pairs, axis=1) + xs * dskip_ref[...]
    y_ref[...] = _rms(y * _silu(z_ref[...]), ng_ref[...]).astype(y_ref.dtype)

    @pl.when(c == nc - 1)
    def _():
        hfin_ref[...] = state_ref[...]
        tailx_ref[...] = tail_x
        tailbc_ref[...] = tail_bc


def _ssd_prompt(x, bc, dt, z, cwx, cbx, cwbc, cbbc, dtb, alog, dskip, ng, batch):
    m, d_ssm = x.shape
    d_bc = bc.shape[1]
    L = SSD_CHUNK
    nc = m // batch // L
    n_heads = d_ssm // HEAD_DIM
    pad = SUBLANES
    head = jnp.arange(LANES, dtype=jnp.int32)[:, None]
    head_lanes = (jnp.arange(d_ssm, dtype=jnp.int32)[None, :] // HEAD_DIM == head).astype(BF16)
    head_cols = (jnp.arange(n_heads * L, dtype=jnp.int32)[None, :] // L == head).astype(BF16)
    row_spec = lambda w: pl.BlockSpec((L, w), lambda b, c: (b * nc + c, 0))
    const = lambda shape: pl.BlockSpec(shape, lambda b, c: (0,) * len(shape))
    state_shape = (n_heads // 2, 2 * HEAD_DIM, D_STATE)
    return pl.pallas_call(
        functools.partial(_ssd_body, nc=nc, n_heads=n_heads),
        grid=(batch, nc),
        in_specs=[
            row_spec(d_ssm), row_spec(d_bc), row_spec(LANES), row_spec(d_ssm),
            const((CONV_W, d_ssm)), const((1, d_ssm)), const((CONV_W, d_bc)), const((1, d_bc)),
            const((1, LANES)), const((1, LANES)), const((1, d_ssm)), const((1, d_ssm)),
            const(head_lanes.shape), const(head_cols.shape),
        ],
        out_specs=[
            row_spec(d_ssm),
            pl.BlockSpec((None,) + state_shape, lambda b, c: (b, 0, 0, 0)),
            pl.BlockSpec((None, pad, d_ssm), lambda b, c: (b, 0, 0)),
            pl.BlockSpec((None, pad, d_bc), lambda b, c: (b, 0, 0)),
        ],
        out_shape=[
            jax.ShapeDtypeStruct((m, d_ssm), BF16),
            jax.ShapeDtypeStruct((batch,) + state_shape, F32),
            jax.ShapeDtypeStruct((batch, pad, d_ssm), F32),
            jax.ShapeDtypeStruct((batch, pad, d_bc), F32),
        ],
        scratch_shapes=[
            pltpu.VMEM((L + pad, d_ssm), F32),
            pltpu.VMEM((L + pad, d_bc), F32),
            pltpu.VMEM(state_shape, F32),
        ],
        compiler_params=_params("parallel", "arbitrary"),
        name="ssd_prompt",
    )(x, bc, dt, z, cwx, cbx, cwbc, cbbc, dtb, alog, dskip, ng, head_lanes, head_cols)


def _expand_heads(v, head_of_lane, n_heads):
    out = jnp.zeros(head_of_lane.shape, F32)
    for h in range(n_heads):
        out = jnp.where(head_of_lane == h, v[:, h:h + 1], out)
    return out


def _rows_to_tile(rows, width):
    rid = lax.broadcasted_iota(jnp.int32, (SUBLANES, width), 0)
    out = jnp.zeros((SUBLANES, width), F32)
    for r, v in enumerate(rows):
        out = jnp.where(rid == r, v, out)
    return out


def _ssd_step_body(x_ref, bc_ref, dt_ref, z_ref, conv_ref, ssm_ref, cw_ref, cb_ref,
                   dtb_ref, alog_ref, dskip_ref, ng_ref,
                   y_ref, conv_out_ref, ssm_out_ref, *, n_heads):
    b = pl.program_id(0)
    N = D_STATE
    d_ssm = x_ref.shape[1]
    half = d_ssm // N_BC_GROUPS

    new = jnp.concatenate([x_ref[pl.ds(b, 1), :], bc_ref[pl.ds(b, 1), :]], axis=1)
    prev = conv_ref[...]
    out = cb_ref[...]
    for j in range(CONV_W - 1):
        out = out + prev[j:j + 1, :] * cw_ref[j:j + 1, :]
    out = out + new * cw_ref[CONV_W - 1:CONV_W, :]
    act = _silu(out)
    for j in range(CONV_W - 2):
        conv_out_ref[j:j + 1, :] = prev[j + 1:j + 2, :]
    conv_out_ref[CONV_W - 2:CONV_W - 1, :] = new

    xs = act[:, :d_ssm]
    gn = N_BC_GROUPS * N
    b_rows = [act[:, d_ssm + g * N:d_ssm + (g + 1) * N] for g in range(N_BC_GROUPS)]
    c_rows = [act[:, d_ssm + gn + g * N:d_ssm + gn + (g + 1) * N] for g in range(N_BC_GROUPS)]

    head_of_lane = lax.broadcasted_iota(jnp.int32, (1, d_ssm), 1) // HEAD_DIM
    group0 = lax.broadcasted_iota(jnp.int32, (1, d_ssm), 1) < half
    dtv = _softplus(dt_ref[pl.ds(b, 1), :] + dtb_ref[...])
    decay = jnp.exp(dtv * -jnp.exp(alog_ref[...]))
    dt_x = _expand_heads(dtv, head_of_lane, n_heads)
    decay_x = _expand_heads(decay, head_of_lane, n_heads)
    xdt = xs * dt_x

    cbs = [jnp.sum(c_rows[g] * b_rows[g], axis=1, keepdims=True) for g in range(N_BC_GROUPS)]
    y_diag = jnp.where(group0, cbs[0], cbs[1]) * xdt

    st = ssm_ref[...]
    st_hi = st.astype(BF16)
    st_lo = (st - st_hi.astype(F32)).astype(BF16)
    c_parts = [_split2(c_rows[g]) for g in range(N_BC_GROUPS)]
    c_tile = _rows_to_tile([c_parts[0][0].astype(F32), c_parts[0][1].astype(F32),
                            c_parts[1][0].astype(F32), c_parts[1][1].astype(F32)], N).astype(BF16)
    r_hi = _dot_nt(c_tile, st_hi)
    r_lo = _dot_nt(c_tile, st_lo)
    off0 = r_hi[0:1, :] + r_hi[1:2, :] + r_lo[0:1, :] + r_lo[1:2, :]
    off1 = r_hi[2:3, :] + r_hi[3:4, :] + r_lo[2:3, :] + r_lo[3:4, :]
    y_off = jnp.where(group0, off0, off1) * decay_x

    y = y_diag + y_off + xs * dskip_ref[...]
    y_ref[...] = _rms(y * _silu(z_ref[pl.ds(b, 1), :]), ng_ref[...])

    parts = [p.astype(F32) for p in _split3(xdt)] + [p.astype(F32) for p in _split3(decay_x)]
    tile = _rows_to_tile(parts, d_ssm).astype(BF16)
    rid = lax.broadcasted_iota(jnp.int32, (SUBLANES, N), 0)
    pick_x = jnp.where(rid < 3, 1.0, 0.0).astype(BF16)
    pick_d = jnp.where((rid >= 3) & (rid < 6), 1.0, 0.0).astype(BF16)
    x_col = _dot_tn(tile, pick_x)
    d_col = _dot_tn(tile, pick_d)
    ssm_out_ref[0:half, :] = d_col[0:half] * st[0:half] + x_col[0:half] * b_rows[0]
    ssm_out_ref[half:, :] = d_col[half:] * st[half:] + x_col[half:] * b_rows[1]


def _ssd_sample(x, bc, dt, z, state_conv, state_ssm, conv_w, conv_b, dtb, alog, dskip, ng, layer, batch):
    d_ssm = x.shape[1]
    n_heads = d_ssm // HEAD_DIM
    conv_dim = state_conv.shape[3]
    rows = n_heads * HEAD_DIM
    whole = lambda a: pl.BlockSpec(a.shape, lambda b: (0,) * a.ndim)
    return pl.pallas_call(
        functools.partial(_ssd_step_body, n_heads=n_heads),
        grid=(batch,),
        in_specs=[
            whole(x), whole(bc), whole(dt), whole(z),
            pl.BlockSpec((None, None, CONV_W - 1, conv_dim), lambda b: (layer, b, 0, 0)),
            pl.BlockSpec((None, None, rows, D_STATE), lambda b: (layer, b, 0, 0)),
            whole(conv_w), whole(conv_b), whole(dtb), whole(alog), whole(dskip), whole(ng),
        ],
        out_specs=[
            pl.BlockSpec((None, 1, d_ssm), lambda b: (b, 0, 0)),
            pl.BlockSpec((None, CONV_W - 1, conv_dim), lambda b: (b, 0, 0)),
            pl.BlockSpec((None, rows, D_STATE), lambda b: (b, 0, 0)),
        ],
        out_shape=[
            jax.ShapeDtypeStruct((batch, 1, d_ssm), F32),
            jax.ShapeDtypeStruct((batch, CONV_W - 1, conv_dim), F32),
            jax.ShapeDtypeStruct((batch, rows, D_STATE), F32),
        ],
        compiler_params=_params("arbitrary"),
        name="ssd_sample",
    )(x, bc, dt, z, state_conv, state_ssm, conv_w, conv_b, dtb, alog, dskip, ng)


def _neg_softplus(z):
    return -(jnp.maximum(z, 0.0) + jnp.log(1.0 + jnp.exp(-jnp.abs(z))))


def _suffix_ones(n):
    row = lax.broadcasted_iota(jnp.int32, (n, n), 0)
    col = lax.broadcasted_iota(jnp.int32, (n, n), 1)
    u = jnp.where(row >= col, 1.0, 0.0).astype(BF16)
    return jnp.concatenate([u, u], axis=0)


def _sb_prompt_body(bias_ref, q_ref, kt_ref, vt_ref, o_ref):
    hp = pl.program_id(1)
    i = pl.program_id(2)
    tq = q_ref.shape[0]
    slab = kt_ref.shape[2]
    blk = ATT_TILE
    nblk = slab // blk
    first = lax.broadcasted_iota(jnp.int32, (tq, LANES), 1) < HEAD_DIM
    q = q_ref[...]
    zero = jnp.zeros_like(q)
    q_heads = (jnp.where(first, q, zero), jnp.where(first, zero, q))
    biases = (bias_ref[2 * hp], bias_ref[2 * hp + 1])
    suffix = _suffix_ones(blk)

    def slab_step(s, carry, mask):
        acc, run = carry
        kt = kt_ref[s]
        z = jnp.concatenate([_dot(q_heads[h], kt) + biases[h] for h in range(2)], axis=0)
        log_keep = _neg_softplus(z)
        if mask is not None:
            log_keep = jnp.where(mask, log_keep, 0.0)
        hi, lo = _split2(log_keep)
        a_blocks = [None] * nblk
        for b in reversed(range(nblk)):
            cols = slice(b * blk, (b + 1) * blk)
            csum = _dot(jnp.concatenate([hi[:, cols], lo[:, cols]], axis=1), suffix)
            a_blocks[b] = jnp.exp(z[:, cols] + csum + run)
            run = run + csum[:, 0:1]
        a = jnp.concatenate(a_blocks, axis=1)
        if mask is not None:
            a = jnp.where(mask, a, 0.0)
        acc = acc + _dot_nt(a.astype(BF16), vt_ref[s])
        return acc, run

    s_diag = (i * tq) // slab
    offset = s_diag * slab - i * tq
    row = lax.broadcasted_iota(jnp.int32, (2 * tq, slab), 0) & (tq - 1)
    col = lax.broadcasted_iota(jnp.int32, (2 * tq, slab), 1)
    carry = (jnp.zeros((2 * tq, LANES), F32), jnp.zeros((2 * tq, 1), F32))
    carry = slab_step(s_diag, carry, col + offset < row)
    acc, _ = lax.fori_loop(0, s_diag, lambda t, c: slab_step(s_diag - 1 - t, c, None), carry)
    o_ref[...] = jnp.where(first, acc[:tq], acc[tq:]).astype(o_ref.dtype)


def _sb_prompt(q, kt_slabs, vt_slabs, bias, batch):
    m, d_att = q.shape
    n_slabs, slab = kt_slabs.shape[1], kt_slabs.shape[3]
    tq = ATT_TILE
    nq = m // batch // tq
    assert tq & (tq - 1) == 0 and slab % tq == 0
    kv_spec = pl.BlockSpec((None, n_slabs, LANES, slab), lambda b, hp, i: (b, 0, hp, 0))
    return pl.pallas_call(
        _sb_prompt_body,
        grid=(batch, d_att // LANES, nq),
        in_specs=[
            pl.BlockSpec(memory_space=pltpu.SMEM),
            pl.BlockSpec((tq, LANES), lambda b, hp, i: (b * nq + i, hp)),
            kv_spec, kv_spec,
        ],
        out_specs=pl.BlockSpec((tq, LANES), lambda b, hp, i: (b * nq + i, hp)),
        out_shape=jax.ShapeDtypeStruct((m, d_att), BF16),
        compiler_params=_params("parallel", "parallel", "arbitrary"),
        name="sb_prompt",
    )(bias, q, kt_slabs, vt_slabs)


def _sb_sample_body(pt_ref, qb_ref, bias_ref, *refs, n_steps, pages):
    k_refs, v_refs = refs[:pages], refs[pages:2 * pages]
    o_ref, run_ref, acc_ref, z_ref, a_ref = refs[2 * pages:]
    j = pl.program_id(1)
    n_heads = qb_ref.shape[0]

    @pl.when(j == 0)
    def _():
        run_ref[...] = jnp.zeros(run_ref.shape, F32)
        acc_ref[...] = jnp.zeros(acc_ref.shape, F32)

    for h in range(n_heads):
        qh = qb_ref[h]
        for p in range(pages):
            z_ref[p, h:h + 1, :] = jnp.sum(k_refs[p][h] * qh, axis=0, keepdims=True)

    suffix = _suffix_ones(PAGE_SIZE)
    run = run_ref[...]
    for p in range(pages):
        z = z_ref[p] + bias_ref[...]
        hi, lo = _split2(_neg_softplus(z))
        csum = _dot(jnp.concatenate([hi, lo], axis=1), suffix)
        a_ref[p] = jnp.exp(z + csum + run)
        run = run + csum[:, 0:1]
    run_ref[...] = run

    for h in range(n_heads):
        acc = acc_ref[h]
        for p in range(pages):
            acc = acc + a_ref[p, h:h + 1, :] * v_refs[p][h]
        acc_ref[h] = acc

    @pl.when(j == n_steps - 1)
    def _():
        rows = acc_ref.shape[0] * acc_ref.shape[1]
        o_ref[...] = jnp.sum(acc_ref[...].reshape(rows, acc_ref.shape[2]), axis=1, keepdims=True)


def _sb_sample(q_lanes, bias_lanes, cache_kt, cache_vt, page_table, layer):
    batch, n_heads, dim, ps = q_lanes.shape
    n_pages = page_table.shape[1]
    pages = SAMPLE_PAGES_PER_STEP
    n_steps = n_pages // pages

    def page_spec(p):
        return pl.BlockSpec((None, None, n_heads, dim, ps),
                            lambda b, j, pt: (layer, pt[b, n_pages - 1 - (j * pages + p)], 0, 0, 0))

    page_specs = [page_spec(p) for p in range(pages)]
    return pl.pallas_call(
        functools.partial(_sb_sample_body, n_steps=n_steps, pages=pages),
        grid_spec=pltpu.PrefetchScalarGridSpec(
            num_scalar_prefetch=1,
            grid=(batch, n_steps),
            in_specs=[pl.BlockSpec((None, n_heads, dim, ps), lambda b, j, pt: (b, 0, 0, 0)),
                      pl.BlockSpec(bias_lanes.shape, lambda b, j, pt: (0, 0))] + page_specs + page_specs,
            out_specs=pl.BlockSpec((None, n_heads * dim, 1), lambda b, j, pt: (b, 0, 0)),
            scratch_shapes=[pltpu.VMEM((n_heads, ps), F32), pltpu.VMEM((n_heads, dim, ps), F32),
                            pltpu.VMEM((pages, n_heads, ps), F32), pltpu.VMEM((pages, n_heads, ps), F32)],
        ),
        out_shape=jax.ShapeDtypeStruct((batch, n_heads * dim, 1), F32),
        compiler_params=_params("arbitrary", "arbitrary"),
        name="sb_sample",
    )(page_table, q_lanes, bias_lanes, *([cache_kt] * pages), *([cache_vt] * pages))


def _pad_lanes(a):
    return jnp.pad(a, ((0, 0), (0, LANES - a.shape[1])))[:, None, :]


def kernel(x_prompt, x_sample, p_prompt, p_sample, cache_k, cache_v, page_table, state_conv, state_ssm, ffn1_pre_g, ffn1_w_gu, ffn1_w_down, ffn1_post_g, mix_pre_g, w_in, conv_w, conv_b, dt_bias, a_log, d_skip, ssm_norm_g, sb_bias, w_out, mix_post_g, ffn2_pre_g, ffn2_w_gu, ffn2_w_down, ffn2_post_g, ple_norm_g, w_ple_gate, w_ple_proj):
    depth = w_in.shape[0]
    bp, seq, d_model = x_prompt.shape
    bs = x_sample.shape[0]
    d_ssm = ssm_norm_g.shape[1]
    n_ssm_heads = d_skip.shape[1]
    n_att_heads = sb_bias.shape[1]
    d_att = n_att_heads * HEAD_DIM
    d_bc = 2 * N_BC_GROUPS * D_STATE
    conv_dim = d_ssm + d_bc
    n_pool = cache_k.shape[1]

    off_xbc = d_ssm
    off_dt = off_xbc + conv_dim
    off_q = off_dt + n_ssm_heads
    off_k = off_q + d_att
    off_v = off_k + d_att

    w_main = jnp.concatenate(
        [w_in[:, :, off_q:], w_in[:, :, :off_xbc], w_in[:, :, off_xbc:off_dt]], axis=2).astype(BF16)
    w_dt = jnp.pad(w_in[:, :, off_dt:off_q], ((0, 0), (0, 0), (0, LANES - n_ssm_heads))).astype(BF16)
    w_gu1, w_dn1 = ffn1_w_gu.astype(BF16), ffn1_w_down.astype(BF16)
    w_gu2, w_dn2 = ffn2_w_gu.astype(BF16), ffn2_w_down.astype(BF16)
    w_o = w_out.astype(BF16)
    w_pg, w_pp = w_ple_gate.astype(BF16), w_ple_proj.astype(BF16)

    row3 = lambda g: g[:, None, :]
    g_f1a, g_f1b = row3(ffn1_pre_g), row3(ffn1_post_g)
    g_f2a, g_f2b = row3(ffn2_pre_g), row3(ffn2_post_g)
    g_mixa, g_mixb = row3(mix_pre_g), row3(mix_post_g)
    g_ple = row3(ple_norm_g)
    dtb, alog = _pad_lanes(dt_bias), _pad_lanes(a_log)
    dskip = jnp.repeat(d_skip, HEAD_DIM, axis=1)[:, None, :]
    ng = row3(ssm_norm_g)
    bias_lanes = jnp.broadcast_to(sb_bias[:, :, None], (depth, n_att_heads, PAGE_SIZE))

    cache_kt = jnp.transpose(cache_k, (0, 1, 3, 4, 2))
    cache_vt = jnp.transpose(cache_v, (0, 1, 3, 4, 2))
    ssm_rows = state_ssm.reshape(depth, bs, n_ssm_heads * HEAD_DIM, D_STATE)

    mp = bp * seq
    hp = x_prompt.reshape(mp, d_model)
    hs = jnp.pad(x_sample.reshape(bs, d_model), ((0, SAMPLE_ROWS - bs), (0, 0)))
    pp = p_prompt.reshape(depth, mp, p_prompt.shape[3])
    psm = jnp.pad(p_sample.reshape(depth, bs, p_sample.shape[3]), ((0, 0), (0, SAMPLE_ROWS - bs), (0, 0)))

    outs = {name: [] for name in ("kp", "vp", "cp", "sp", "ks", "vs", "cs", "ss")}
    for i in range(depth):
        hp = _ffn(hp, g_f1a, w_gu1, w_dn1, g_f1b, i, ROW_TILE)
        q, kt, vt, kt_slabs, vt_slabs, z, x, bc, dt = _proj_cols(hp, g_mixa, w_main, w_dt, i, bp)
        y, hfin, tail_x, tail_bc = _ssd_prompt(
            x, bc, dt, z,
            conv_w[i][:, :d_ssm], conv_b[i][None, :d_ssm], conv_w[i][:, d_ssm:], conv_b[i][None, d_ssm:],
            dtb[i], alog[i], dskip[i], ng[i], bp)
        o = _sb_prompt(q, kt_slabs, vt_slabs, sb_bias[i], bp)
        hp = _outproj(hp, y, o, w_o, g_mixb, i, ROW_TILE)
        hp = _ffn(hp, g_f2a, w_gu2, w_dn2, g_f2b, i, ROW_TILE)
        hp = _ple(hp, pp, g_ple, w_pg, w_pp, i, ROW_TILE)
        keep = SUBLANES - (CONV_W - 1)
        token_major = lambda t: jnp.transpose(t.reshape(bp, n_att_heads, HEAD_DIM, seq), (0, 3, 1, 2))
        outs["kp"].append(token_major(kt))
        outs["vp"].append(token_major(vt))
        outs["cp"].append(jnp.concatenate([tail_x[:, keep:], tail_bc[:, keep:]], axis=2))
        outs["sp"].append(hfin.reshape(bp, n_ssm_heads, HEAD_DIM, D_STATE))

        hs = _ffn(hs, g_f1a, w_gu1, w_dn1, g_f1b, i, SAMPLE_ROWS)
        q, k, v, z, x, bc, dt = _proj_rows(hs, g_mixa, w_main, w_dt, i)
        y, conv_new, ssm_new = _ssd_sample(
            x, bc, dt, z, state_conv, ssm_rows, conv_w[i], conv_b[i][None, :],
            dtb[i], alog[i], dskip[i], ng[i], i, bs)
        q_lanes = jnp.broadcast_to(q[:bs].reshape(bs, n_att_heads, HEAD_DIM, 1),
                                   (bs, n_att_heads, HEAD_DIM, PAGE_SIZE))
        o = _sb_sample(q_lanes, bias_lanes[i], cache_kt, cache_vt, page_table, i)
        pad_rows = lambda a: jnp.pad(a.reshape(bs, -1), ((0, SAMPLE_ROWS - bs), (0, 0)))
        hs = _outproj(hs, pad_rows(y), pad_rows(o), w_o, g_mixb, i, SAMPLE_ROWS)
        hs = _ffn(hs, g_f2a, w_gu2, w_dn2, g_f2b, i, SAMPLE_ROWS)
        hs = _ple(hs, psm, g_ple, w_pg, w_pp, i, SAMPLE_ROWS)
        outs["ks"].append(k[:bs].reshape(bs, 1, n_att_heads, HEAD_DIM))
        outs["vs"].append(v[:bs].reshape(bs, 1, n_att_heads, HEAD_DIM))
        outs["cs"].append(conv_new)
        outs["ss"].append(ssm_new.reshape(bs, n_ssm_heads, HEAD_DIM, D_STATE))

    stack = lambda name: jnp.stack(outs[name])
    return (hp.reshape(bp, seq, d_model), hs[:bs].reshape(bs, 1, d_model),
            stack("kp"), stack("vp"), stack("cp"), stack("sp"),
            stack("ks"), stack("vs"), stack("cs"), stack("ss"))
```

```python
import functools
import math

import jax
import jax.numpy as jnp
from jax import lax
from jax.experimental import pallas as pl
from jax.experimental.pallas import tpu as pltpu

BF16 = jnp.bfloat16
F32 = jnp.float32

HEAD_DIM = 64
D_STATE = 128
N_BC_GROUPS = 2
CONV_W = 4
SSD_CHUNK = 128
PAGE_SIZE = 128
RMS_EPS = 1e-6

LANES = 128
SUBLANES = 8
VMEM_LIMIT_BYTES = 56 * 1024 * 1024

ROW_TILE = 512
COL_TILE = 512
WIDE_TILE = 1024
ATT_TILE = 256
SAMPLE_ROWS = 16
SAMPLE_PAGES_PER_STEP = 8


def _params(*semantics):
    return pltpu.CompilerParams(dimension_semantics=semantics,
                                vmem_limit_bytes=VMEM_LIMIT_BYTES)


def _rms(x, g):
    ms = jnp.mean(x * x, axis=-1, keepdims=True)
    return x * lax.rsqrt(ms + RMS_EPS) * g


def _silu(x):
    return x * jax.nn.sigmoid(x)


def _softplus(x):
    return jnp.maximum(x, 0.0) + jnp.log1p(jnp.exp(-jnp.abs(x)))


def _dot(a, b):
    return jnp.dot(a, b, preferred_element_type=F32)


def _dot_nt(a, b):
    return lax.dot_general(a, b, (((1,), (1,)), ((), ())), preferred_element_type=F32)


def _dot_tn(a, b):
    return lax.dot_general(a, b, (((0,), (0,)), ((), ())), preferred_element_type=F32)


def _split2(x):
    hi = x.astype(BF16)
    lo = (x - hi.astype(F32)).astype(BF16)
    return hi, lo


def _split3(x):
    hi = x.astype(BF16)
    r = x - hi.astype(F32)
    mid = r.astype(BF16)
    lo = (r - mid.astype(F32)).astype(BF16)
    return hi, mid, lo


def _ffn_body(h_ref, pre_ref, wg_ref, wu_ref, wd_ref, post_ref, o_ref, u_ref, acc_ref, *, nf):
    j = pl.program_id(1)

    @pl.when(j == 0)
    def _():
        u_ref[...] = _rms(h_ref[...], pre_ref[...]).astype(BF16)
        acc_ref[...] = jnp.zeros(acc_ref.shape, F32)

    u = u_ref[...]
    g = _dot(u, wg_ref[...])
    up = _dot(u, wu_ref[...])
    a = (_silu(g) * up).astype(BF16)
    acc_ref[...] += _dot(a, wd_ref[...])

    @pl.when(j == nf - 1)
    def _():
        o_ref[...] = h_ref[...] + 0.5 * _rms(acc_ref[...], post_ref[...])


def _ffn(h, pre_g, w_gu, w_down, post_g, layer, tm):
    m, d = h.shape
    f = w_down.shape[1]
    tf = COL_TILE
    nf = f // tf
    return pl.pallas_call(
        functools.partial(_ffn_body, nf=nf),
        grid=(m // tm, nf),
        in_specs=[
            pl.BlockSpec((tm, d), lambda i, j: (i, 0)),
            pl.BlockSpec((None, 1, d), lambda i, j: (layer, 0, 0)),
            pl.BlockSpec((None, d, tf), lambda i, j: (layer, 0, j)),
            pl.BlockSpec((None, d, tf), lambda i, j: (layer, 0, j + nf)),
            pl.BlockSpec((None, tf, d), lambda i, j: (layer, j, 0)),
            pl.BlockSpec((None, 1, d), lambda i, j: (layer, 0, 0)),
        ],
        out_specs=pl.BlockSpec((tm, d), lambda i, j: (i, 0)),
        out_shape=jax.ShapeDtypeStruct((m, d), F32),
        scratch_shapes=[pltpu.VMEM((tm, d), BF16), pltpu.VMEM((tm, d), F32)],
        compiler_params=_params("parallel", "arbitrary"),
        name="ffn",
    )(h, pre_g, w_gu, w_gu, w_down, post_g)


_PROJ_SEGMENTS = ((0, 1), (1, 1), (2, 1), (3, 1), (4, 1))
_Q_SCALE = 1.0 / math.sqrt(HEAD_DIM)


def _in_segment(j, seg):
    start, count = seg
    return (j >= start) & (j < start + count)


def _proj_common(h_ref, g_ref, w_ref, wx_ref, dt_ref, u_ref, z_ref, x_ref, bc_ref):
    j = pl.program_id(1)

    @pl.when(j == 0)
    def _():
        u = _rms(h_ref[...], g_ref[...]).astype(BF16)
        u_ref[...] = u
        extra = _dot(u, wx_ref[...])
        d_bc = bc_ref.shape[1]
        bc_ref[...] = extra[:, :d_bc]
        dt_ref[...] = extra[:, d_bc:]

    r = _dot(u_ref[...], w_ref[...])
    for ref, seg in zip((z_ref, x_ref), _PROJ_SEGMENTS[3:]):
        @pl.when(_in_segment(j, seg))
        def _(ref=ref):
            ref[...] = r
    return j, r


def _proj_rows_body(h_ref, g_ref, w_ref, wdt_ref, q_ref, k_ref, v_ref, z_ref, x_ref, bc_ref, dt_ref, u_ref):
    j, r = _proj_common(h_ref, g_ref, w_ref, wdt_ref, dt_ref, u_ref, z_ref, x_ref, bc_ref)

    @pl.when(_in_segment(j, _PROJ_SEGMENTS[0]))
    def _():
        q_ref[...] = r * _Q_SCALE

    for ref, seg in zip((k_ref, v_ref), _PROJ_SEGMENTS[1:3]):
        @pl.when(_in_segment(j, seg))
        def _(ref=ref):
            ref[...] = r


def _proj_cols_body(h_ref, g_ref, w_ref, wdt_ref, q_ref, kt_ref, vt_ref, ktb_ref, vtb_ref,
                    z_ref, x_ref, bc_ref, dt_ref, u_ref):
    j, r = _proj_common(h_ref, g_ref, w_ref, wdt_ref, dt_ref, u_ref, z_ref, x_ref, bc_ref)

    @pl.when(_in_segment(j, _PROJ_SEGMENTS[0]))
    def _():
        q_ref[...] = (r * _Q_SCALE).astype(q_ref.dtype)

    for ref, bref, seg in zip((kt_ref, vt_ref), (ktb_ref, vtb_ref), _PROJ_SEGMENTS[1:3]):
        @pl.when(_in_segment(j, seg))
        def _(ref=ref, bref=bref):
            rt = r.T
            ref[...] = rt
            bref[...] = rt.astype(BF16)


def _proj_in_specs(tm, d, tn, d_extra, layer):
    return [
        pl.BlockSpec((tm, d), lambda i, j: (i, 0)),
        pl.BlockSpec((None, 1, d), lambda i, j: (layer, 0, 0)),
        pl.BlockSpec((None, d, tn), lambda i, j: (layer, 0, j)),
        pl.BlockSpec((None, d, d_extra), lambda i, j: (layer, 0, 0)),
    ]


def _proj_rows(h, g, w_main, w_extra, layer):
    m, d = h.shape
    tn = WIDE_TILE
    d_bc = w_extra.shape[2] - LANES
    widths = [tn] * len(_PROJ_SEGMENTS) + [d_bc, LANES]
    return pl.pallas_call(
        _proj_rows_body,
        grid=(1, w_main.shape[2] // tn),
        in_specs=_proj_in_specs(m, d, tn, w_extra.shape[2], layer),
        out_specs=[pl.BlockSpec((m, w), lambda i, j: (i, 0)) for w in widths],
        out_shape=[jax.ShapeDtypeStruct((m, w), F32) for w in widths],
        scratch_shapes=[pltpu.VMEM((m, d), BF16)],
        compiler_params=_params("parallel", "arbitrary"),
        name="proj_rows",
    )(h, g, w_main, w_extra)


def _proj_cols(h, g, w_main, w_extra, layer, batch):
    m, d = h.shape
    tn = WIDE_TILE
    tm = ROW_TILE
    seq = m // batch
    tiles_per_seq = seq // tm
    d_bc = w_extra.shape[2] - LANES
    row_spec = lambda w: pl.BlockSpec((tm, w), lambda i, j: (i, 0))
    t_spec = pl.BlockSpec((None, tn, tm), lambda i, j: (i // tiles_per_seq, 0, i % tiles_per_seq))
    slab_spec = pl.BlockSpec((None, None, tn, tm), lambda i, j: (i // tiles_per_seq, i % tiles_per_seq, 0, 0))
    out_specs = [row_spec(tn), t_spec, t_spec, slab_spec, slab_spec,
                 row_spec(tn), row_spec(tn), row_spec(d_bc), row_spec(LANES)]
    out_shape = [
        jax.ShapeDtypeStruct((m, tn), BF16),
        jax.ShapeDtypeStruct((batch, tn, seq), F32),
        jax.ShapeDtypeStruct((batch, tn, seq), F32),
        jax.ShapeDtypeStruct((batch, tiles_per_seq, tn, tm), BF16),
        jax.ShapeDtypeStruct((batch, tiles_per_seq, tn, tm), BF16),
        jax.ShapeDtypeStruct((m, tn), F32),
        jax.ShapeDtypeStruct((m, tn), F32),
        jax.ShapeDtypeStruct((m, d_bc), F32),
        jax.ShapeDtypeStruct((m, LANES), F32),
    ]
    return pl.pallas_call(
        _proj_cols_body,
        grid=(m // tm, w_main.shape[2] // tn),
        in_specs=_proj_in_specs(tm, d, tn, w_extra.shape[2], layer),
        out_specs=out_specs,
        out_shape=out_shape,
        scratch_shapes=[pltpu.VMEM((tm, d), BF16)],
        compiler_params=_params("parallel", "arbitrary"),
        name="proj_cols",
    )(h, g, w_main, w_extra)


def _outproj_body(h_ref, y_ref, a_ref, wy_ref, wa_ref, g_ref, o_ref, mix_ref, *, nn, tn):
    j = pl.program_id(1)
    mix_ref[j] = (_dot(y_ref[...].astype(BF16), wy_ref[...])
                  + _dot(a_ref[...].astype(BF16), wa_ref[...]))

    @pl.when(j == nn - 1)
    def _():
        ss = jnp.zeros((h_ref.shape[0], 1), F32)
        for t in range(nn):
            mt = mix_ref[t]
            ss = ss + jnp.sum(mt * mt, axis=-1, keepdims=True)
        rs = lax.rsqrt(ss / (nn * tn) + RMS_EPS)
        for t in range(nn):
            cols = slice(t * tn, (t + 1) * tn)
            o_ref[:, cols] = h_ref[:, cols] + mix_ref[t] * rs * g_ref[:, cols]


def _outproj(h, y, a, w_out, g, layer, tm):
    m, d = h.shape
    dy = y.shape[1]
    da = a.shape[1]
    tn = WIDE_TILE
    nn = d // tn
    return pl.pallas_call(
        functools.partial(_outproj_body, nn=nn, tn=tn),
        grid=(m // tm, nn),
        in_specs=[
            pl.BlockSpec((tm, d), lambda i, j: (i, 0)),
            pl.BlockSpec((tm, dy), lambda i, j: (i, 0)),
            pl.BlockSpec((tm, da), lambda i, j: (i, 0)),
            pl.BlockSpec((None, dy, tn), lambda i, j: (layer, 0, j)),
            pl.BlockSpec((None, da, tn), lambda i, j: (layer, dy // da, j)),
            pl.BlockSpec((None, 1, d), lambda i, j: (layer, 0, 0)),
        ],
        out_specs=pl.BlockSpec((tm, d), lambda i, j: (i, 0)),
        out_shape=jax.ShapeDtypeStruct((m, d), F32),
        scratch_shapes=[pltpu.VMEM((nn, tm, tn), F32)],
        compiler_params=_params("parallel", "arbitrary"),
        name="outproj",
    )(h, y, a, w_out, w_out, g)


def _ple_body(h_ref, hcol_ref, p_ref, g_ref, wg_ref, wp_ref, o_ref, u_ref):
    j = pl.program_id(1)

    @pl.when(j == 0)
    def _():
        u_ref[...] = _rms(h_ref[...], g_ref[...]).astype(BF16)

    gate = jax.nn.sigmoid(_dot(u_ref[...], wg_ref[...]))
    o_ref[...] = hcol_ref[...] + gate * _dot(p_ref[...].astype(BF16), wp_ref[...])


def _ple(h, p, g, w_gate, w_proj, layer, tm):
    m, d = h.shape
    dp = p.shape[2]
    tn = WIDE_TILE
    return pl.pallas_call(
        _ple_body,
        grid=(m // tm, d // tn),
        in_specs=[
            pl.BlockSpec((tm, d), lambda i, j: (i, 0)),
            pl.BlockSpec((tm, tn), lambda i, j: (i, j)),
            pl.BlockSpec((None, tm, dp), lambda i, j: (layer, i, 0)),
            pl.BlockSpec((None, 1, d), lambda i, j: (layer, 0, 0)),
            pl.BlockSpec((None, d, tn), lambda i, j: (layer, 0, j)),
            pl.BlockSpec((None, dp, tn), lambda i, j: (layer, 0, j)),
        ],
        out_specs=pl.BlockSpec((tm, tn), lambda i, j: (i, j)),
        out_shape=jax.ShapeDtypeStruct((m, d), F32),
        scratch_shapes=[pltpu.VMEM((tm, d), BF16)],
        compiler_params=_params("parallel", "arbitrary"),
        name="ple",
    )(h, h, p, g, w_gate, w_proj)


def _ssd_body(x_ref, bc_ref, dt_ref, z_ref, cwx_ref, cbx_ref, cwbc_ref, cbbc_ref,
              dtb_ref, alog_ref, dskip_ref, ng_ref, hexp_ref, hcol_ref,
              y_ref, hfin_ref, tailx_ref, tailbc_ref,
              xpad_ref, bcpad_ref, state_ref, *, nc, n_heads):
    c = pl.program_id(1)
    L = SSD_CHUNK
    P = HEAD_DIM
    N = D_STATE
    pairs_per_group = n_heads // N_BC_GROUPS // 2
    pad = SUBLANES

    @pl.when(c == 0)
    def _():
        xpad_ref[0:pad, :] = jnp.zeros((pad, xpad_ref.shape[1]), F32)
        bcpad_ref[0:pad, :] = jnp.zeros((pad, bcpad_ref.shape[1]), F32)
        state_ref[...] = jnp.zeros(state_ref.shape, F32)

    xpad_ref[pad:pad + L, :] = x_ref[...]
    bcpad_ref[pad:pad + L, :] = bc_ref[...]

    def conv(pad_ref, w_ref, b_ref):
        ext = pad_ref[...]
        out = b_ref[...]
        for j in range(CONV_W):
            back = CONV_W - 1 - j
            tap = ext if back == 0 else pltpu.roll(ext, back, axis=0)
            out = out + tap[pad:pad + L] * w_ref[j:j + 1, :]
        return _silu(out)

    xs = conv(xpad_ref, cwx_ref, cbx_ref)
    bcs = conv(bcpad_ref, cwbc_ref, cbbc_ref)

    tail_x = xpad_ref[L:L + pad, :]
    tail_bc = bcpad_ref[L:L + pad, :]
    xpad_ref[0:pad, :] = tail_x
    bcpad_ref[0:pad, :] = tail_bc

    def spread(v, ones_ref):
        return sum(_dot(part, ones_ref[...]) for part in _split3(v))

    dtv = _softplus(dt_ref[...] + dtb_ref[...])
    da = dtv * -jnp.exp(alog_ref[...])

    row = lax.broadcasted_iota(jnp.int32, (L, L), 0)
    col = lax.broadcasted_iota(jnp.int32, (L, L), 1)
    causal = row >= col
    tril = jnp.where(causal, 1.0, 0.0).astype(BF16)
    cum = sum(_dot(tril, part) for part in _split3(da))
    cum_t = cum.T
    cum_cols = spread(cum, hcol_ref)
    cum_x = spread(cum, hexp_ref)
    dt_x = spread(dtv, hexp_ref)
    cum_last_x = cum_x[L - 1:L, :]
    ecum_x = jnp.exp(cum_x)
    chunk_decay_x = jnp.exp(cum_last_x)
    xdt = xs * dt_x
    xdt_bf = xdt.astype(BF16)
    xdt_to_end = (xdt * jnp.exp(cum_last_x - cum_x)).astype(BF16)

    gn = N_BC_GROUPS * N
    b_groups = [bcs[:, g * N:(g + 1) * N].astype(BF16) for g in range(N_BC_GROUPS)]
    c_groups = [bcs[:, gn + g * N:gn + (g + 1) * N].astype(BF16) for g in range(N_BC_GROUPS)]
    cb = [_dot_nt(c_groups[g], b_groups[g]) for g in range(N_BC_GROUPS)]

    first = lax.broadcasted_iota(jnp.int32, (L, 2 * P), 1) < P
    first_rows = lax.broadcasted_iota(jnp.int32, (2 * P, N), 0) < P
    zero_bf = jnp.zeros((L, 2 * P), BF16)
    y_pairs = []
    for pr in range(n_heads // 2):
        g = pr // pairs_per_group
        lanes = slice(pr * 2 * P, (pr + 1) * 2 * P)
        weights = []
        for h in (2 * pr, 2 * pr + 1):
            diff = cum_cols[:, h * L:(h + 1) * L] - cum_t[h:h + 1, :]
            decay = jnp.exp(jnp.where(causal, diff, -jnp.inf))
            weights.append((cb[g] * decay).astype(BF16))
        xp = xdt_bf[:, lanes]
        y_diag = _dot(jnp.concatenate(weights, axis=1),
                      jnp.concatenate([jnp.where(first, xp, zero_bf), jnp.where(first, zero_bf, xp)], axis=0))
        st = state_ref[pr]
        y_off = _dot_nt(c_groups[g], st.astype(BF16)) * ecum_x[:, lanes]
        y_pairs.append(y_diag + y_off)
        upd = _dot_tn(xdt_to_end[:, lanes], b_groups[g])
        cd = chunk_decay_x[:, lanes]
        state_ref[pr] = jnp.where(first_rows, cd[:, 0:1], cd[:, P:P + 1]) * st + upd

    y = jnp.concatenate(y_pairs, axis=1) + xs * dskip_ref[...]
    y_ref[...] = _rms(y * _silu(z_ref[...]), ng_ref[...]).astype(y_ref.dtype)

    @pl.when(c == nc - 1)
    def _():
        hfin_ref[...] = state_ref[...]
        tailx_ref[...] = tail_x
        tailbc_ref[...] = tail_bc


def _ssd_prompt(x, bc, dt, z, cwx, cbx, cwbc, cbbc, dtb, alog, dskip, ng, batch):
    m, d_ssm = x.shape
    d_bc = bc.shape[1]
    L = SSD_CHUNK
    nc = m // batch // L
    n_heads = d_ssm // HEAD_DIM
    pad = SUBLANES
    head = jnp.arange(LANES, dtype=jnp.int32)[:, None]
    head_lanes = (jnp.arange(d_ssm, dtype=jnp.int32)[None, :] // HEAD_DIM == head).astype(BF16)
    head_cols = (jnp.arange(n_heads * L, dtype=jnp.int32)[None, :] // L == head).astype(BF16)
    row_spec = lambda w: pl.BlockSpec((L, w), lambda b, c: (b * nc + c, 0))
    const = lambda shape: pl.BlockSpec(shape, lambda b, c: (0,) * len(shape))
    state_shape = (n_heads // 2, 2 * HEAD_DIM, D_STATE)
    return pl.pallas_call(
        functools.partial(_ssd_body, nc=nc, n_heads=n_heads),
        grid=(batch, nc),
        in_specs=[
            row_spec(d_ssm), row_spec(d_bc), row_spec(LANES), row_spec(d_ssm),
            const((CONV_W, d_ssm)), const((1, d_ssm)), const((CONV_W, d_bc)), const((1, d_bc)),
            const((1, LANES)), const((1, LANES)), const((1, d_ssm)), const((1, d_ssm)),
            const(head_lanes.shape), const(head_cols.shape),
        ],
        out_specs=[
            row_spec(d_ssm),
            pl.BlockSpec((None,) + state_shape, lambda b, c: (b, 0, 0, 0)),
            pl.BlockSpec((None, pad, d_ssm), lambda b, c: (b, 0, 0)),
            pl.BlockSpec((None, pad, d_bc), lambda b, c: (b, 0, 0)),
        ],
        out_shape=[
            jax.ShapeDtypeStruct((m, d_ssm), BF16),
            jax.ShapeDtypeStruct((batch,) + state_shape, F32),
            jax.ShapeDtypeStruct((batch, pad, d_ssm), F32),
            jax.ShapeDtypeStruct((batch, pad, d_bc), F32),
        ],
        scratch_shapes=[
            pltpu.VMEM((L + pad, d_ssm), F32),
            pltpu.VMEM((L + pad, d_bc), F32),
            pltpu.VMEM(state_shape, F32),
        ],
        compiler_params=_params("parallel", "arbitrary"),
        name="ssd_prompt",
    )(x, bc, dt, z, cwx, cbx, cwbc, cbbc, dtb, alog, dskip, ng, head_lanes, head_cols)


def _expand_heads(v, head_of_lane, n_heads):
    out = jnp.zeros(head_of_lane.shape, F32)
    for h in range(n_heads):
        out = jnp.where(head_of_lane == h, v[:, h:h + 1], out)
    return out


def _rows_to_tile(rows, width):
    rid = lax.broadcasted_iota(jnp.int32, (SUBLANES, width), 0)
    out = jnp.zeros((SUBLANES, width), F32)
    for r, v in enumerate(rows):
        out = jnp.where(rid == r, v, out)
    return out


def _ssd_step_body(x_ref, bc_ref, dt_ref, z_ref, conv_ref, ssm_ref, cw_ref, cb_ref,
                   dtb_ref, alog_ref, dskip_ref, ng_ref,
                   y_ref, conv_out_ref, ssm_out_ref, *, n_heads):
    b = pl.program_id(0)
    N = D_STATE
    d_ssm = x_ref.shape[1]
    half = d_ssm // N_BC_GROUPS

    new = jnp.concatenate([x_ref[pl.ds(b, 1), :], bc_ref[pl.ds(b, 1), :]], axis=1)
    prev = conv_ref[...]
    out = cb_ref[...]
    for j in range(CONV_W - 1):
        out = out + prev[j:j + 1, :] * cw_ref[j:j + 1, :]
    out = out + new * cw_ref[CONV_W - 1:CONV_W, :]
    act = _silu(out)
    for j in range(CONV_W - 2):
        conv_out_ref[j:j + 1, :] = prev[j + 1:j + 2, :]
    conv_out_ref[CONV_W - 2:CONV_W - 1, :] = new

    xs = act[:, :d_ssm]
    gn = N_BC_GROUPS * N
    b_rows = [act[:, d_ssm + g * N:d_ssm + (g + 1) * N] for g in range(N_BC_GROUPS)]
    c_rows = [act[:, d_ssm + gn + g * N:d_ssm + gn + (g + 1) * N] for g in range(N_BC_GROUPS)]

    head_of_lane = lax.broadcasted_iota(jnp.int32, (1, d_ssm), 1) // HEAD_DIM
    group0 = lax.broadcasted_iota(jnp.int32, (1, d_ssm), 1) < half
    dtv = _softplus(dt_ref[pl.ds(b, 1), :] + dtb_ref[...])
    decay = jnp.exp(dtv * -jnp.exp(alog_ref[...]))
    dt_x = _expand_heads(dtv, head_of_lane, n_heads)
    decay_x = _expand_heads(decay, head_of_lane, n_heads)
    xdt = xs * dt_x

    cbs = [jnp.sum(c_rows[g] * b_rows[g], axis=1, keepdims=True) for g in range(N_BC_GROUPS)]
    y_diag = jnp.where(group0, cbs[0], cbs[1]) * xdt

    st = ssm_ref[...]
    st_hi = st.astype(BF16)
    st_lo = (st - st_hi.astype(F32)).astype(BF16)
    c_parts = [_split2(c_rows[g]) for g in range(N_BC_GROUPS)]
    c_tile = _rows_to_tile([c_parts[0][0].astype(F32), c_parts[0][1].astype(F32),
                            c_parts[1][0].astype(F32), c_parts[1][1].astype(F32)], N).astype(BF16)
    r_hi = _dot_nt(c_tile, st_hi)
    r_lo = _dot_nt(c_tile, st_lo)
    off0 = r_hi[0:1, :] + r_hi[1:2, :] + r_lo[0:1, :] + r_lo[1:2, :]
    off1 = r_hi[2:3, :] + r_hi[3:4, :] + r_lo[2:3, :] + r_lo[3:4, :]
    y_off = jnp.where(group0, off0, off1) * decay_x

    y = y_diag + y_off + xs * dskip_ref[...]
    y_ref[...] = _rms(y * _silu(z_ref[pl.ds(b, 1), :]), ng_ref[...])

    parts = [p.astype(F32) for p in _split3(xdt)] + [p.astype(F32) for p in _split3(decay_x)]
    tile = _rows_to_tile(parts, d_ssm).astype(BF16)
    rid = lax.broadcasted_iota(jnp.int32, (SUBLANES, N), 0)
    pick_x = jnp.where(rid < 3, 1.0, 0.0).astype(BF16)
    pick_d = jnp.where((rid >= 3) & (rid < 6), 1.0, 0.0).astype(BF16)
    x_col = _dot_tn(tile, pick_x)
    d_col = _dot_tn(tile, pick_d)
    ssm_out_ref[0:half, :] = d_col[0:half] * st[0:half] + x_col[0:half] * b_rows[0]
    ssm_out_ref[half:, :] = d_col[half:] * st[half:] + x_col[half:] * b_rows[1]


def _ssd_sample(x, bc, dt, z, state_conv, state_ssm, conv_w, conv_b, dtb, alog, dskip, ng, layer, batch):
    d_ssm = x.shape[1]
    n_heads = d_ssm // HEAD_DIM
    conv_dim = state_conv.shape[3]
    rows = n_heads * HEAD_DIM
    whole = lambda a: pl.BlockSpec(a.shape, lambda b: (0,) * a.ndim)
    return pl.pallas_call(
        functools.partial(_ssd_step_body, n_heads=n_heads),
        grid=(batch,),
        in_specs=[
            whole(x), whole(bc), whole(dt), whole(z),
            pl.BlockSpec((None, None, CONV_W - 1, conv_dim), lambda b: (layer, b, 0, 0)),
            pl.BlockSpec((None, None, rows, D_STATE), lambda b: (layer, b, 0, 0)),
            whole(conv_w), whole(conv_b), whole(dtb), whole(alog), whole(dskip), whole(ng),
        ],
        out_specs=[
            pl.BlockSpec((None, 1, d_ssm), lambda b: (b, 0, 0)),
            pl.BlockSpec((None, CONV_W - 1, conv_dim), lambda b: (b, 0, 0)),
            pl.BlockSpec((None, rows, D_STATE), lambda b: (b, 0, 0)),
        ],
        out_shape=[
            jax.ShapeDtypeStruct((batch, 1, d_ssm), F32),
            jax.ShapeDtypeStruct((batch, CONV_W - 1, conv_dim), F32),
            jax.ShapeDtypeStruct((batch, rows, D_STATE), F32),
        ],
        compiler_params=_params("arbitrary"),
        name="ssd_sample",
    )(x, bc, dt, z, state_conv, state_ssm, conv_w, conv_b, dtb, alog, dskip, ng)


def _log_drop(z):
    neg_abs = pltpu.bitcast(pltpu.bitcast(z, jnp.uint32) | jnp.uint32(0x80000000), F32)
    return jnp.maximum(z, 0.0) + jnp.log(1.0 + jnp.exp(neg_abs))


def _suffix_ones(n):
    row = lax.broadcasted_iota(jnp.int32, (n, n), 0)
    col = lax.broadcasted_iota(jnp.int32, (n, n), 1)
    u = jnp.where(row >= col, 1.0, 0.0).astype(BF16)
    return jnp.concatenate([u, u], axis=0)


def _sb_prompt_body(bias_ref, q_ref, kt_ref, vt_ref, o_ref, *, nq):
    hp = pl.program_id(1)
    i = pl.program_id(2)
    tq = q_ref.shape[0]
    slab = kt_ref.shape[2]
    blk = ATT_TILE
    nblk = slab // blk
    first = lax.broadcasted_iota(jnp.int32, (tq, LANES), 1) < HEAD_DIM
    q = q_ref[...]
    zero = jnp.zeros_like(q)
    q_heads = (jnp.where(first, q, zero), jnp.where(first, zero, q))
    biases = (bias_ref[2 * hp], bias_ref[2 * hp + 1])
    suffix = _suffix_ones(blk)

    def score(s, mask):
        kt = kt_ref[s]
        z = jnp.concatenate([_dot(q_heads[h], kt) + biases[h] for h in range(2)], axis=0)
        drop = _log_drop(z)
        if mask is not None:
            drop = jnp.where(mask, drop, 0.0)
            z = jnp.where(mask, z, -jnp.inf)
        hi, lo = _split2(drop)
        return z, hi, lo

    def fold(s, scored, carry):
        z, hi, lo = scored
        acc, run = carry
        a_blocks = [None] * nblk
        for b in reversed(range(nblk)):
            cols = slice(b * blk, (b + 1) * blk)
            csum = _dot(jnp.concatenate([hi[:, cols], lo[:, cols]], axis=1), suffix)
            a_blocks[b] = jnp.exp(z[:, cols] - csum - run).astype(BF16)
            run = run + csum[:, 0:1]
        acc = acc + _dot_nt(jnp.concatenate(a_blocks, axis=1), vt_ref[s])
        return acc, run

    for n in range(nq):
        @pl.when(i == n)
        def _(n=n):
            row = lax.broadcasted_iota(jnp.int32, (2 * tq, slab), 0) & (tq - 1)
            col = lax.broadcasted_iota(jnp.int32, (2 * tq, slab), 1)
            pending = score(n, col < row)
            carry = (jnp.zeros((2 * tq, LANES), F32), jnp.zeros((2 * tq, 1), F32))
            for s in range(n - 1, -1, -1):
                scored = score(s, None)
                carry = fold(s + 1, pending, carry)
                pending = scored
            acc, _ = fold(0, pending, carry)
            o_ref[...] = jnp.where(first, acc[:tq], acc[tq:]).astype(o_ref.dtype)


def _sb_prompt(q, kt_slabs, vt_slabs, bias, batch):
    m, d_att = q.shape
    n_slabs, slab = kt_slabs.shape[1], kt_slabs.shape[3]
    tq = slab
    nq = m // batch // tq
    assert tq & (tq - 1) == 0 and slab % ATT_TILE == 0
    kv_spec = pl.BlockSpec((None, n_slabs, LANES, slab), lambda b, hp, i: (b, 0, hp, 0))
    return pl.pallas_call(
        functools.partial(_sb_prompt_body, nq=nq),
        grid=(batch, d_att // LANES, nq),
        in_specs=[
            pl.BlockSpec(memory_space=pltpu.SMEM),
            pl.BlockSpec((tq, LANES), lambda b, hp, i: (b * nq + i, hp)),
            kv_spec, kv_spec,
        ],
        out_specs=pl.BlockSpec((tq, LANES), lambda b, hp, i: (b * nq + i, hp)),
        out_shape=jax.ShapeDtypeStruct((m, d_att), BF16),
        compiler_params=_params("parallel", "parallel", "arbitrary"),
        name="sb_prompt",
    )(bias, q, kt_slabs, vt_slabs)


def _sb_sample_body(pt_ref, qb_ref, bias_ref, *refs, n_steps, pages):
    k_refs, v_refs = refs[:pages], refs[pages:2 * pages]
    o_ref, run_ref, acc_ref, z_ref, a_ref = refs[2 * pages:]
    j = pl.program_id(1)
    n_heads = qb_ref.shape[0]

    @pl.when(j == 0)
    def _():
        run_ref[...] = jnp.zeros(run_ref.shape, F32)
        acc_ref[...] = jnp.zeros(acc_ref.shape, F32)

    for h in range(n_heads):
        qh = qb_ref[h]
        for p in range(pages):
            z_ref[p, h:h + 1, :] = jnp.sum(k_refs[p][h] * qh, axis=0, keepdims=True)

    suffix = _suffix_ones(PAGE_SIZE)
    run = run_ref[...]
    for p in range(pages):
        z = z_ref[p] + bias_ref[...]
        hi, lo = _split2(_log_drop(z))
        csum = _dot(jnp.concatenate([hi, lo], axis=1), suffix)
        a_ref[p] = jnp.exp(z - csum - run)
        run = run + csum[:, 0:1]
    run_ref[...] = run

    for h in range(n_heads):
        acc = acc_ref[h]
        for p in range(pages):
            acc = acc + a_ref[p, h:h + 1, :] * v_refs[p][h]
        acc_ref[h] = acc

    @pl.when(j == n_steps - 1)
    def _():
        rows = acc_ref.shape[0] * acc_ref.shape[1]
        o_ref[...] = jnp.sum(acc_ref[...].reshape(rows, acc_ref.shape[2]), axis=1, keepdims=True)


def _sb_sample(q_lanes, bias_lanes, cache_kt, cache_vt, page_table, layer):
    batch, n_heads, dim, ps = q_lanes.shape
    n_pages = page_table.shape[1]
    pages = SAMPLE_PAGES_PER_STEP
    n_steps = n_pages // pages

    def page_spec(p):
        return pl.BlockSpec((None, None, n_heads, dim, ps),
                            lambda b, j, pt: (layer, pt[b, n_pages - 1 - (j * pages + p)], 0, 0, 0))

    page_specs = [page_spec(p) for p in range(pages)]
    return pl.pallas_call(
        functools.partial(_sb_sample_body, n_steps=n_steps, pages=pages),
        grid_spec=pltpu.PrefetchScalarGridSpec(
            num_scalar_prefetch=1,
            grid=(batch, n_steps),
            in_specs=[pl.BlockSpec((None, n_heads, dim, ps), lambda b, j, pt: (b, 0, 0, 0)),
                      pl.BlockSpec(bias_lanes.shape, lambda b, j, pt: (0, 0))] + page_specs + page_specs,
            out_specs=pl.BlockSpec((None, n_heads * dim, 1), lambda b, j, pt: (b, 0, 0)),
            scratch_shapes=[pltpu.VMEM((n_heads, ps), F32), pltpu.VMEM((n_heads, dim, ps), F32),
                            pltpu.VMEM((pages, n_heads, ps), F32), pltpu.VMEM((pages, n_heads, ps), F32)],
        ),
        out_shape=jax.ShapeDtypeStruct((batch, n_heads * dim, 1), F32),
        compiler_params=_params("arbitrary", "arbitrary"),
        name="sb_sample",
    )(page_table, q_lanes, bias_lanes, *([cache_kt] * pages), *([cache_vt] * pages))


def _pad_lanes(a):
    return jnp.pad(a, ((0, 0), (0, LANES - a.shape[1])))[:, None, :]


def kernel(x_prompt, x_sample, p_prompt, p_sample, cache_k, cache_v, page_table, state_conv, state_ssm, ffn1_pre_g, ffn1_w_gu, ffn1_w_down, ffn1_post_g, mix_pre_g, w_in, conv_w, conv_b, dt_bias, a_log, d_skip, ssm_norm_g, sb_bias, w_out, mix_post_g, ffn2_pre_g, ffn2_w_gu, ffn2_w_down, ffn2_post_g, ple_norm_g, w_ple_gate, w_ple_proj):
    depth = w_in.shape[0]
    bp, seq, d_model = x_prompt.shape
    bs = x_sample.shape[0]
    d_ssm = ssm_norm_g.shape[1]
    n_ssm_heads = d_skip.shape[1]
    n_att_heads = sb_bias.shape[1]
    d_att = n_att_heads * HEAD_DIM
    d_bc = 2 * N_BC_GROUPS * D_STATE
    conv_dim = d_ssm + d_bc
    n_pool = cache_k.shape[1]

    off_xbc = d_ssm
    off_dt = off_xbc + conv_dim
    off_q = off_dt + n_ssm_heads
    off_k = off_q + d_att
    off_v = off_k + d_att

    assert d_att == WIDE_TILE and d_ssm == WIDE_TILE
    w_main = jnp.concatenate(
        [w_in[:, :, off_q:], w_in[:, :, :off_xbc], w_in[:, :, off_xbc:off_xbc + d_ssm]], axis=2).astype(BF16)
    w_extra = jnp.pad(w_in[:, :, off_xbc + d_ssm:off_q],
                      ((0, 0), (0, 0), (0, LANES - n_ssm_heads))).astype(BF16)
    w_gu1, w_dn1 = ffn1_w_gu.astype(BF16), ffn1_w_down.astype(BF16)
    w_gu2, w_dn2 = ffn2_w_gu.astype(BF16), ffn2_w_down.astype(BF16)
    w_o = w_out.astype(BF16)
    w_pg, w_pp = w_ple_gate.astype(BF16), w_ple_proj.astype(BF16)

    row3 = lambda g: g[:, None, :]
    g_f1a, g_f1b = row3(ffn1_pre_g), row3(ffn1_post_g)
    g_f2a, g_f2b = row3(ffn2_pre_g), row3(ffn2_post_g)
    g_mixa, g_mixb = row3(mix_pre_g), row3(mix_post_g)
    g_ple = row3(ple_norm_g)
    dtb, alog = _pad_lanes(dt_bias), _pad_lanes(a_log)
    dskip = jnp.repeat(d_skip, HEAD_DIM, axis=1)[:, None, :]
    ng = row3(ssm_norm_g)
    bias_lanes = jnp.broadcast_to(sb_bias[:, :, None], (depth, n_att_heads, PAGE_SIZE))

    cache_kt = jnp.transpose(cache_k, (0, 1, 3, 4, 2))
    cache_vt = jnp.transpose(cache_v, (0, 1, 3, 4, 2))
    ssm_rows = state_ssm.reshape(depth, bs, n_ssm_heads * HEAD_DIM, D_STATE)

    mp = bp * seq
    hp = x_prompt.reshape(mp, d_model)
    hs = jnp.pad(x_sample.reshape(bs, d_model), ((0, SAMPLE_ROWS - bs), (0, 0)))
    pp = p_prompt.reshape(depth, mp, p_prompt.shape[3])
    psm = jnp.pad(p_sample.reshape(depth, bs, p_sample.shape[3]), ((0, 0), (0, SAMPLE_ROWS - bs), (0, 0)))

    outs = {name: [] for name in ("kp", "vp", "cp", "sp", "ks", "vs", "cs", "ss")}
    for i in range(depth):
        hp = _ffn(hp, g_f1a, w_gu1, w_dn1, g_f1b, i, ROW_TILE)
        q, kt, vt, kt_slabs, vt_slabs, z, x, bc, dt = _proj_cols(hp, g_mixa, w_main, w_extra, i, bp)
        y, hfin, tail_x, tail_bc = _ssd_prompt(
            x, bc, dt, z,
            conv_w[i][:, :d_ssm], conv_b[i][None, :d_ssm], conv_w[i][:, d_ssm:], conv_b[i][None, d_ssm:],
            dtb[i], alog[i], dskip[i], ng[i], bp)
        o = _sb_prompt(q, kt_slabs, vt_slabs, sb_bias[i], bp)
        hp = _outproj(hp, y, o, w_o, g_mixb, i, ROW_TILE)
        hp = _ffn(hp, g_f2a, w_gu2, w_dn2, g_f2b, i, ROW_TILE)
        hp = _ple(hp, pp, g_ple, w_pg, w_pp, i, ROW_TILE)
        keep = SUBLANES - (CONV_W - 1)
        token_major = lambda t: jnp.transpose(t.reshape(bp, n_att_heads, HEAD_DIM, seq), (0, 3, 1, 2))
        outs["kp"].append(token_major(kt))
        outs["vp"].append(token_major(vt))
        outs["cp"].append(jnp.concatenate([tail_x[:, keep:], tail_bc[:, keep:]], axis=2))
        outs["sp"].append(hfin.reshape(bp, n_ssm_heads, HEAD_DIM, D_STATE))

        hs = _ffn(hs, g_f1a, w_gu1, w_dn1, g_f1b, i, SAMPLE_ROWS)
        q, k, v, z, x, bc, dt = _proj_rows(hs, g_mixa, w_main, w_extra, i)
        y, conv_new, ssm_new = _ssd_sample(
            x, bc, dt, z, state_conv, ssm_rows, conv_w[i], conv_b[i][None, :],
            dtb[i], alog[i], dskip[i], ng[i], i, bs)
        q_lanes = jnp.broadcast_to(q[:bs].reshape(bs, n_att_heads, HEAD_DIM, 1),
                                   (bs, n_att_heads, HEAD_DIM, PAGE_SIZE))
        o = _sb_sample(q_lanes, bias_lanes[i], cache_kt, cache_vt, page_table, i)
        pad_rows = lambda a: jnp.pad(a.reshape(bs, -1), ((0, SAMPLE_ROWS - bs), (0, 0)))
        hs = _outproj(hs, pad_rows(y), pad_rows(o), w_o, g_mixb, i, SAMPLE_ROWS)
        hs = _ffn(hs, g_f2a, w_gu2, w_dn2, g_f2b, i, SAMPLE_ROWS)
        hs = _ple(hs, psm, g_ple, w_pg, w_pp, i, SAMPLE_ROWS)
        outs["ks"].append(k[:bs].reshape(bs, 1, n_att_heads, HEAD_DIM))
        outs["vs"].append(v[:bs].reshape(bs, 1, n_att_heads, HEAD_DIM))
        outs["cs"].append(conv_new)
        outs["ss"].append(ssm_new.reshape(bs, n_ssm_heads, HEAD_DIM, D_STATE))

    stack = lambda name: jnp.stack(outs[name])
    return (hp.reshape(bp, seq, d_model), hs[:bs].reshape(bs, 1, d_model),
            stack("kp"), stack("vp"), stack("cp"), stack("sp"),
            stack("ks"), stack("vs"), stack("cs"), stack("ss"))
```

```python
import functools
import math

import jax
import jax.numpy as jnp
from jax import lax
from jax.experimental import pallas as pl
from jax.experimental.pallas import tpu as pltpu

BF16 = jnp.bfloat16
F32 = jnp.float32

HEAD_DIM = 64
D_STATE = 128
N_BC_GROUPS = 2
CONV_W = 4
SSD_CHUNK = 128
PAGE_SIZE = 128
RMS_EPS = 1e-6

LANES = 128
SUBLANES = 8
VMEM_LIMIT_BYTES = 56 * 1024 * 1024

ROW_TILE = 512
RESIDENT_ROW_TILE = 256
COL_TILE = 512
WIDE_TILE = 1024
ATT_TILE = 256
SAMPLE_ROWS = 16
SAMPLE_PAGES_PER_STEP = 8


def _params(*semantics):
    return pltpu.CompilerParams(dimension_semantics=semantics,
                                vmem_limit_bytes=VMEM_LIMIT_BYTES)


def _rms(x, g):
    ms = jnp.mean(x * x, axis=-1, keepdims=True)
    return x * lax.rsqrt(ms + RMS_EPS) * g


def _silu(x):
    return x * jax.nn.sigmoid(x)


def _softplus(x):
    return jnp.maximum(x, 0.0) + jnp.log1p(jnp.exp(-jnp.abs(x)))


def _dot(a, b):
    return jnp.dot(a, b, preferred_element_type=F32)


def _dot_nt(a, b):
    return lax.dot_general(a, b, (((1,), (1,)), ((), ())), preferred_element_type=F32)


def _dot_tn(a, b):
    return lax.dot_general(a, b, (((0,), (0,)), ((), ())), preferred_element_type=F32)


def _split2(x):
    hi = x.astype(BF16)
    lo = (x - hi.astype(F32)).astype(BF16)
    return hi, lo


def _split3(x):
    hi = x.astype(BF16)
    r = x - hi.astype(F32)
    mid = r.astype(BF16)
    lo = (r - mid.astype(F32)).astype(BF16)
    return hi, mid, lo


def _ffn_body(h_ref, pre_ref, wg_ref, wu_ref, wd_ref, post_ref, o_ref, u_ref, acc_ref, *, nf):
    j = pl.program_id(1)

    @pl.when(j == 0)
    def _():
        u_ref[...] = _rms(h_ref[...], pre_ref[...]).astype(BF16)
        acc_ref[...] = jnp.zeros(acc_ref.shape, F32)

    u = u_ref[...]
    g = _dot(u, wg_ref[...])
    up = _dot(u, wu_ref[...])
    a = (_silu(g) * up).astype(BF16)
    acc_ref[...] += _dot(a, wd_ref[...])

    @pl.when(j == nf - 1)
    def _():
        o_ref[...] = h_ref[...] + 0.5 * _rms(acc_ref[...], post_ref[...])


def _ffn(h, pre_g, w_gu, w_down, post_g, layer, tm):
    m, d = h.shape
    f = w_down.shape[1]
    tf = COL_TILE
    nf = f // tf
    return pl.pallas_call(
        functools.partial(_ffn_body, nf=nf),
        grid=(m // tm, nf),
        in_specs=[
            pl.BlockSpec((tm, d), lambda i, j: (i, 0)),
            pl.BlockSpec((None, 1, d), lambda i, j: (layer, 0, 0)),
            pl.BlockSpec((None, d, tf), lambda i, j: (layer, 0, j)),
            pl.BlockSpec((None, d, tf), lambda i, j: (layer, 0, j + nf)),
            pl.BlockSpec((None, tf, d), lambda i, j: (layer, j, 0)),
            pl.BlockSpec((None, 1, d), lambda i, j: (layer, 0, 0)),
        ],
        out_specs=pl.BlockSpec((tm, d), lambda i, j: (i, 0)),
        out_shape=jax.ShapeDtypeStruct((m, d), F32),
        scratch_shapes=[pltpu.VMEM((tm, d), BF16), pltpu.VMEM((tm, d), F32)],
        compiler_params=_params("parallel", "arbitrary"),
        name="ffn",
    )(h, pre_g, w_gu, w_gu, w_down, post_g)


_PROJ_SEGMENTS = ((0, 1), (1, 1), (2, 1), (3, 1), (4, 1))
_Q_SCALE = 1.0 / math.sqrt(HEAD_DIM)


def _in_segment(j, seg):
    start, count = seg
    return (j >= start) & (j < start + count)


def _proj_common(h_ref, g_ref, w_ref, wx_ref, dt_ref, u_ref, z_ref, x_ref, bc_ref):
    j = pl.program_id(1)

    @pl.when(j == 0)
    def _():
        u = _rms(h_ref[...], g_ref[...]).astype(BF16)
        u_ref[...] = u
        extra = _dot(u, wx_ref[...])
        d_bc = bc_ref.shape[1]
        bc_ref[...] = extra[:, :d_bc]
        dt_ref[...] = extra[:, d_bc:]

    r = _dot(u_ref[...], w_ref[...])
    for ref, seg in zip((z_ref, x_ref), _PROJ_SEGMENTS[3:]):
        @pl.when(_in_segment(j, seg))
        def _(ref=ref):
            ref[...] = r
    return j, r


def _proj_rows_body(h_ref, g_ref, w_ref, wdt_ref, q_ref, k_ref, v_ref, z_ref, x_ref, bc_ref, dt_ref, u_ref):
    j, r = _proj_common(h_ref, g_ref, w_ref, wdt_ref, dt_ref, u_ref, z_ref, x_ref, bc_ref)

    @pl.when(_in_segment(j, _PROJ_SEGMENTS[0]))
    def _():
        q_ref[...] = r * _Q_SCALE

    for ref, seg in zip((k_ref, v_ref), _PROJ_SEGMENTS[1:3]):
        @pl.when(_in_segment(j, seg))
        def _(ref=ref):
            ref[...] = r


def _proj_cols_body(h_ref, g_ref, w_ref, wdt_ref, q_ref, kt_ref, vt_ref, ktb_ref, vtb_ref,
                    z_ref, x_ref, bc_ref, dt_ref, u_ref):
    j, r = _proj_common(h_ref, g_ref, w_ref, wdt_ref, dt_ref, u_ref, z_ref, x_ref, bc_ref)

    @pl.when(_in_segment(j, _PROJ_SEGMENTS[0]))
    def _():
        q_ref[...] = (r * _Q_SCALE).astype(q_ref.dtype)

    for ref, bref, seg in zip((kt_ref, vt_ref), (ktb_ref, vtb_ref), _PROJ_SEGMENTS[1:3]):
        @pl.when(_in_segment(j, seg))
        def _(ref=ref, bref=bref):
            rt = r.T
            ref[...] = rt
            bref[...] = rt.astype(BF16)


def _proj_in_specs(tm, d, tn, d_extra, layer):
    return [
        pl.BlockSpec((tm, d), lambda i, j: (i, 0)),
        pl.BlockSpec((None, 1, d), lambda i, j: (layer, 0, 0)),
        pl.BlockSpec((None, d, tn), lambda i, j: (layer, 0, j)),
        pl.BlockSpec((None, d, d_extra), lambda i, j: (layer, 0, 0)),
    ]


def _proj_rows(h, g, w_main, w_extra, layer):
    m, d = h.shape
    tn = WIDE_TILE
    d_bc = w_extra.shape[2] - LANES
    widths = [tn] * len(_PROJ_SEGMENTS) + [d_bc, LANES]
    return pl.pallas_call(
        _proj_rows_body,
        grid=(1, w_main.shape[2] // tn),
        in_specs=_proj_in_specs(m, d, tn, w_extra.shape[2], layer),
        out_specs=[pl.BlockSpec((m, w), lambda i, j: (i, 0)) for w in widths],
        out_shape=[jax.ShapeDtypeStruct((m, w), F32) for w in widths],
        scratch_shapes=[pltpu.VMEM((m, d), BF16)],
        compiler_params=_params("parallel", "arbitrary"),
        name="proj_rows",
    )(h, g, w_main, w_extra)


def _proj_cols(h, g, w_main, w_extra, layer, batch):
    m, d = h.shape
    tn = WIDE_TILE
    tm = ROW_TILE
    seq = m // batch
    tiles_per_seq = seq // tm
    d_bc = w_extra.shape[2] - LANES
    row_spec = lambda w: pl.BlockSpec((tm, w), lambda i, j: (i, 0))
    t_spec = pl.BlockSpec((None, tn, tm), lambda i, j: (i // tiles_per_seq, 0, i % tiles_per_seq))
    slab_spec = pl.BlockSpec((None, None, tn, tm), lambda i, j: (i // tiles_per_seq, i % tiles_per_seq, 0, 0))
    out_specs = [row_spec(tn), t_spec, t_spec, slab_spec, slab_spec,
                 row_spec(tn), row_spec(tn), row_spec(d_bc), row_spec(LANES)]
    out_shape = [
        jax.ShapeDtypeStruct((m, tn), BF16),
        jax.ShapeDtypeStruct((batch, tn, seq), F32),
        jax.ShapeDtypeStruct((batch, tn, seq), F32),
        jax.ShapeDtypeStruct((batch, tiles_per_seq, tn, tm), BF16),
        jax.ShapeDtypeStruct((batch, tiles_per_seq, tn, tm), BF16),
        jax.ShapeDtypeStruct((m, tn), F32),
        jax.ShapeDtypeStruct((m, tn), F32),
        jax.ShapeDtypeStruct((m, d_bc), F32),
        jax.ShapeDtypeStruct((m, LANES), F32),
    ]
    return pl.pallas_call(
        _proj_cols_body,
        grid=(m // tm, w_main.shape[2] // tn),
        in_specs=_proj_in_specs(tm, d, tn, w_extra.shape[2], layer),
        out_specs=out_specs,
        out_shape=out_shape,
        scratch_shapes=[pltpu.VMEM((tm, d), BF16)],
        compiler_params=_params("parallel", "arbitrary"),
        name="proj_cols",
    )(h, g, w_main, w_extra)


def _outproj_body(h_ref, y_ref, a_ref, wy_ref, wa_ref, g_ref, o_ref):
    mix = _dot(y_ref[...].astype(BF16), wy_ref[...]) + _dot(a_ref[...].astype(BF16), wa_ref[...])
    o_ref[...] = h_ref[...] + _rms(mix, g_ref[...])


def _outproj(h, y, a, w_out, g, layer, tm):
    m, d = h.shape
    dy = y.shape[1]
    da = a.shape[1]
    return pl.pallas_call(
        _outproj_body,
        grid=(m // tm,),
        in_specs=[
            pl.BlockSpec((tm, d), lambda i: (i, 0)),
            pl.BlockSpec((tm, dy), lambda i: (i, 0)),
            pl.BlockSpec((tm, da), lambda i: (i, 0)),
            pl.BlockSpec((None, dy, d), lambda i: (layer, 0, 0)),
            pl.BlockSpec((None, da, d), lambda i: (layer, dy // da, 0)),
            pl.BlockSpec((None, 1, d), lambda i: (layer, 0, 0)),
        ],
        out_specs=pl.BlockSpec((tm, d), lambda i: (i, 0)),
        out_shape=jax.ShapeDtypeStruct((m, d), F32),
        compiler_params=_params("parallel"),
        name="outproj",
    )(h, y, a, w_out, w_out, g)


def _ple_body(h_ref, p_ref, g_ref, wg_ref, wp_ref, o_ref):
    h = h_ref[...]
    gate = jax.nn.sigmoid(_dot(_rms(h, g_ref[...]).astype(BF16), wg_ref[...]))
    o_ref[...] = h + gate * _dot(p_ref[...].astype(BF16), wp_ref[...])


def _ple(h, p, g, w_gate, w_proj, layer, tm):
    m, d = h.shape
    dp = p.shape[2]
    return pl.pallas_call(
        _ple_body,
        grid=(m // tm,),
        in_specs=[
            pl.BlockSpec((tm, d), lambda i: (i, 0)),
            pl.BlockSpec((None, tm, dp), lambda i: (layer, i, 0)),
            pl.BlockSpec((None, 1, d), lambda i: (layer, 0, 0)),
            pl.BlockSpec((None, d, d), lambda i: (layer, 0, 0)),
            pl.BlockSpec((None, dp, d), lambda i: (layer, 0, 0)),
        ],
        out_specs=pl.BlockSpec((tm, d), lambda i: (i, 0)),
        out_shape=jax.ShapeDtypeStruct((m, d), F32),
        compiler_params=_params("parallel"),
        name="ple",
    )(h, p, g, w_gate, w_proj)


def _ssd_body(x_ref, bc_ref, dt_ref, z_ref, cwx_ref, cbx_ref, cwbc_ref, cbbc_ref,
              dtb_ref, alog_ref, dskip_ref, ng_ref, hexp_ref, hcol_ref,
              y_ref, hfin_ref, tailx_ref, tailbc_ref,
              xpad_ref, bcpad_ref, state_ref, *, nc, n_heads):
    c = pl.program_id(1)
    L = SSD_CHUNK
    P = HEAD_DIM
    N = D_STATE
    pairs_per_group = n_heads // N_BC_GROUPS // 2
    pad = SUBLANES

    @pl.when(c == 0)
    def _():
        xpad_ref[0:pad, :] = jnp.zeros((pad, xpad_ref.shape[1]), F32)
        bcpad_ref[0:pad, :] = jnp.zeros((pad, bcpad_ref.shape[1]), F32)
        state_ref[...] = jnp.zeros(state_ref.shape, F32)

    xpad_ref[pad:pad + L, :] = x_ref[...]
    bcpad_ref[pad:pad + L, :] = bc_ref[...]

    def conv(pad_ref, w_ref, b_ref):
        ext = pad_ref[...]
        out = b_ref[...]
        for j in range(CONV_W):
            back = CONV_W - 1 - j
            tap = ext if back == 0 else pltpu.roll(ext, back, axis=0)
            out = out + tap[pad:pad + L] * w_ref[j:j + 1, :]
        return _silu(out)

    xs = conv(xpad_ref, cwx_ref, cbx_ref)
    bcs = conv(bcpad_ref, cwbc_ref, cbbc_ref)

    tail_x = xpad_ref[L:L + pad, :]
    tail_bc = bcpad_ref[L:L + pad, :]
    xpad_ref[0:pad, :] = tail_x
    bcpad_ref[0:pad, :] = tail_bc

    def spread(v, ones_ref):
        return sum(_dot(part, ones_ref[...]) for part in _split3(v))

    dtv = _softplus(dt_ref[...] + dtb_ref[...])
    da = dtv * -jnp.exp(alog_ref[...])

    row = lax.broadcasted_iota(jnp.int32, (L, L), 0)
    col = lax.broadcasted_iota(jnp.int32, (L, L), 1)
    causal = row >= col
    tril = jnp.where(causal, 1.0, 0.0).astype(BF16)
    cum = sum(_dot(tril, part) for part in _split3(da))
    cum_t = cum.T
    cum_cols = spread(cum, hcol_ref)
    cum_x = spread(cum, hexp_ref)
    dt_x = spread(dtv, hexp_ref)
    cum_last_x = cum_x[L - 1:L, :]
    ecum_x = jnp.exp(cum_x)
    chunk_decay_x = jnp.exp(cum_last_x)
    xdt = xs * dt_x
    xdt_bf = xdt.astype(BF16)
    xdt_to_end = (xdt * jnp.exp(cum_last_x - cum_x)).astype(BF16)

    gn = N_BC_GROUPS * N
    b_groups = [bcs[:, g * N:(g + 1) * N].astype(BF16) for g in range(N_BC_GROUPS)]
    c_groups = [bcs[:, gn + g * N:gn + (g + 1) * N].astype(BF16) for g in range(N_BC_GROUPS)]
    cb = [_dot_nt(c_groups[g], b_groups[g]) for g in range(N_BC_GROUPS)]

    first = lax.broadcasted_iota(jnp.int32, (L, 2 * P), 1) < P
    first_rows = lax.broadcasted_iota(jnp.int32, (2 * P, N), 0) < P
    zero_bf = jnp.zeros((L, 2 * P), BF16)
    y_pairs = []
    for pr in range(n_heads // 2):
        g = pr // pairs_per_group
        lanes = slice(pr * 2 * P, (pr + 1) * 2 * P)
        weights = []
        for h in (2 * pr, 2 * pr + 1):
            diff = cum_cols[:, h * L:(h + 1) * L] - cum_t[h:h + 1, :]
            decay = jnp.exp(jnp.where(causal, diff, -jnp.inf))
            weights.append((cb[g] * decay).astype(BF16))
        xp = xdt_bf[:, lanes]
        y_diag = _dot(jnp.concatenate(weights, axis=1),
                      jnp.concatenate([jnp.where(first, xp, zero_bf), jnp.where(first, zero_bf, xp)], axis=0))
        st = state_ref[pr]
        y_off = _dot_nt(c_groups[g], st.astype(BF16)) * ecum_x[:, lanes]
        y_pairs.append(y_diag + y_off)
        upd = _dot_tn(xdt_to_end[:, lanes], b_groups[g])
        cd = chunk_decay_x[:, lanes]
        state_ref[pr] = jnp.where(first_rows, cd[:, 0:1], cd[:, P:P + 1]) * st + upd

    y = jnp.concatenate(y_pairs, axis=1) + xs * dskip_ref[...]
    y_ref[...] = _rms(y * _silu(z_ref[...]), ng_ref[...]).astype(y_ref.dtype)

    @pl.when(c == nc - 1)
    def _():
        hfin_ref[...] = state_ref[...]
        tailx_ref[...] = tail_x
        tailbc_ref[...] = tail_bc


def _ssd_prompt(x, bc, dt, z, cwx, cbx, cwbc, cbbc, dtb, alog, dskip, ng, batch):
    m, d_ssm = x.shape
    d_bc = bc.shape[1]
    L = SSD_CHUNK
    nc = m // batch // L
    n_heads = d_ssm // HEAD_DIM
    pad = SUBLANES
    head = jnp.arange(LANES, dtype=jnp.int32)[:, None]
    head_lanes = (jnp.arange(d_ssm, dtype=jnp.int32)[None, :] // HEAD_DIM == head).astype(BF16)
    head_cols = (jnp.arange(n_heads * L, dtype=jnp.int32)[None, :] // L == head).astype(BF16)
    row_spec = lambda w: pl.BlockSpec((L, w), lambda b, c: (b * nc + c, 0))
    const = lambda shape: pl.BlockSpec(shape, lambda b, c: (0,) * len(shape))
    state_shape = (n_heads // 2, 2 * HEAD_DIM, D_STATE)
    return pl.pallas_call(
        functools.partial(_ssd_body, nc=nc, n_heads=n_heads),
        grid=(batch, nc),
        in_specs=[
            row_spec(d_ssm), row_spec(d_bc), row_spec(LANES), row_spec(d_ssm),
            const((CONV_W, d_ssm)), const((1, d_ssm)), const((CONV_W, d_bc)), const((1, d_bc)),
            const((1, LANES)), const((1, LANES)), const((1, d_ssm)), const((1, d_ssm)),
            const(head_lanes.shape), const(head_cols.shape),
        ],
        out_specs=[
            row_spec(d_ssm),
            pl.BlockSpec((None,) + state_shape, lambda b, c: (b, 0, 0, 0)),
            pl.BlockSpec((None, pad, d_ssm), lambda b, c: (b, 0, 0)),
            pl.BlockSpec((None, pad, d_bc), lambda b, c: (b, 0, 0)),
        ],
        out_shape=[
            jax.ShapeDtypeStruct((m, d_ssm), BF16),
            jax.ShapeDtypeStruct((batch,) + state_shape, F32),
            jax.ShapeDtypeStruct((batch, pad, d_ssm), F32),
            jax.ShapeDtypeStruct((batch, pad, d_bc), F32),
        ],
        scratch_shapes=[
            pltpu.VMEM((L + pad, d_ssm), F32),
            pltpu.VMEM((L + pad, d_bc), F32),
            pltpu.VMEM(state_shape, F32),
        ],
        compiler_params=_params("parallel", "arbitrary"),
        name="ssd_prompt",
    )(x, bc, dt, z, cwx, cbx, cwbc, cbbc, dtb, alog, dskip, ng, head_lanes, head_cols)


def _expand_heads(v, head_of_lane, n_heads):
    out = jnp.zeros(head_of_lane.shape, F32)
    for h in range(n_heads):
        out = jnp.where(head_of_lane == h, v[:, h:h + 1], out)
    return out


def _rows_to_tile(rows, width):
    rid = lax.broadcasted_iota(jnp.int32, (SUBLANES, width), 0)
    out = jnp.zeros((SUBLANES, width), F32)
    for r, v in enumerate(rows):
        out = jnp.where(rid == r, v, out)
    return out


def _ssd_step_body(x_ref, bc_ref, dt_ref, z_ref, conv_ref, ssm_ref, cw_ref, cb_ref,
                   dtb_ref, alog_ref, dskip_ref, ng_ref,
                   y_ref, conv_out_ref, ssm_out_ref, *, n_heads):
    b = pl.program_id(0)
    N = D_STATE
    d_ssm = x_ref.shape[1]
    half = d_ssm // N_BC_GROUPS

    new = jnp.concatenate([x_ref[pl.ds(b, 1), :], bc_ref[pl.ds(b, 1), :]], axis=1)
    prev = conv_ref[...]
    out = cb_ref[...]
    for j in range(CONV_W - 1):
        out = out + prev[j:j + 1, :] * cw_ref[j:j + 1, :]
    out = out + new * cw_ref[CONV_W - 1:CONV_W, :]
    act = _silu(out)
    for j in range(CONV_W - 2):
        conv_out_ref[j:j + 1, :] = prev[j + 1:j + 2, :]
    conv_out_ref[CONV_W - 2:CONV_W - 1, :] = new

    xs = act[:, :d_ssm]
    gn = N_BC_GROUPS * N
    b_rows = [act[:, d_ssm + g * N:d_ssm + (g + 1) * N] for g in range(N_BC_GROUPS)]
    c_rows = [act[:, d_ssm + gn + g * N:d_ssm + gn + (g + 1) * N] for g in range(N_BC_GROUPS)]

    head_of_lane = lax.broadcasted_iota(jnp.int32, (1, d_ssm), 1) // HEAD_DIM
    group0 = lax.broadcasted_iota(jnp.int32, (1, d_ssm), 1) < half
    dtv = _softplus(dt_ref[pl.ds(b, 1), :] + dtb_ref[...])
    decay = jnp.exp(dtv * -jnp.exp(alog_ref[...]))
    dt_x = _expand_heads(dtv, head_of_lane, n_heads)
    decay_x = _expand_heads(decay, head_of_lane, n_heads)
    xdt = xs * dt_x

    cbs = [jnp.sum(c_rows[g] * b_rows[g], axis=1, keepdims=True) for g in range(N_BC_GROUPS)]
    y_diag = jnp.where(group0, cbs[0], cbs[1]) * xdt

    st = ssm_ref[...]
    st_hi = st.astype(BF16)
    st_lo = (st - st_hi.astype(F32)).astype(BF16)
    c_parts = [_split2(c_rows[g]) for g in range(N_BC_GROUPS)]
    c_tile = _rows_to_tile([c_parts[0][0].astype(F32), c_parts[0][1].astype(F32),
                            c_parts[1][0].astype(F32), c_parts[1][1].astype(F32)], N).astype(BF16)
    r_hi = _dot_nt(c_tile, st_hi)
    r_lo = _dot_nt(c_tile, st_lo)
    off0 = r_hi[0:1, :] + r_hi[1:2, :] + r_lo[0:1, :] + r_lo[1:2, :]
    off1 = r_hi[2:3, :] + r_hi[3:4, :] + r_lo[2:3, :] + r_lo[3:4, :]
    y_off = jnp.where(group0, off0, off1) * decay_x

    y = y_diag + y_off + xs * dskip_ref[...]
    y_ref[...] = _rms(y * _silu(z_ref[pl.ds(b, 1), :]), ng_ref[...])

    parts = [p.astype(F32) for p in _split3(xdt)] + [p.astype(F32) for p in _split3(decay_x)]
    tile = _rows_to_tile(parts, d_ssm).astype(BF16)
    rid = lax.broadcasted_iota(jnp.int32, (SUBLANES, N), 0)
    pick_x = jnp.where(rid < 3, 1.0, 0.0).astype(BF16)
    pick_d = jnp.where((rid >= 3) & (rid < 6), 1.0, 0.0).astype(BF16)
    x_col = _dot_tn(tile, pick_x)
    d_col = _dot_tn(tile, pick_d)
    ssm_out_ref[0:half, :] = d_col[0:half] * st[0:half] + x_col[0:half] * b_rows[0]
    ssm_out_ref[half:, :] = d_col[half:] * st[half:] + x_col[half:] * b_rows[1]


def _ssd_sample(x, bc, dt, z, state_conv, state_ssm, conv_w, conv_b, dtb, alog, dskip, ng, layer, batch):
    d_ssm = x.shape[1]
    n_heads = d_ssm // HEAD_DIM
    conv_dim = state_conv.shape[3]
    rows = n_heads * HEAD_DIM
    whole = lambda a: pl.BlockSpec(a.shape, lambda b: (0,) * a.ndim)
    return pl.pallas_call(
        functools.partial(_ssd_step_body, n_heads=n_heads),
        grid=(batch,),
        in_specs=[
            whole(x), whole(bc), whole(dt), whole(z),
            pl.BlockSpec((None, None, CONV_W - 1, conv_dim), lambda b: (layer, b, 0, 0)),
            pl.BlockSpec((None, None, rows, D_STATE), lambda b: (layer, b, 0, 0)),
            whole(conv_w), whole(conv_b), whole(dtb), whole(alog), whole(dskip), whole(ng),
        ],
        out_specs=[
            pl.BlockSpec((None, 1, d_ssm), lambda b: (b, 0, 0)),
            pl.BlockSpec((None, CONV_W - 1, conv_dim), lambda b: (b, 0, 0)),
            pl.BlockSpec((None, rows, D_STATE), lambda b: (b, 0, 0)),
        ],
        out_shape=[
            jax.ShapeDtypeStruct((batch, 1, d_ssm), F32),
            jax.ShapeDtypeStruct((batch, CONV_W - 1, conv_dim), F32),
            jax.ShapeDtypeStruct((batch, rows, D_STATE), F32),
        ],
        compiler_params=_params("arbitrary"),
        name="ssd_sample",
    )(x, bc, dt, z, state_conv, state_ssm, conv_w, conv_b, dtb, alog, dskip, ng)


def _log_drop(z):
    return jnp.maximum(z, 0.0) + jnp.log(1.0 + jnp.exp(-jnp.abs(z)))


def _suffix_ones(n):
    row = lax.broadcasted_iota(jnp.int32, (n, n), 0)
    col = lax.broadcasted_iota(jnp.int32, (n, n), 1)
    u = jnp.where(row >= col, 1.0, 0.0).astype(BF16)
    return jnp.concatenate([u, u], axis=0)


def _sb_prompt_body(bias_ref, q_ref, kt_ref, vt_ref, o_ref, *, nq):
    hp = pl.program_id(1)
    i = pl.program_id(2)
    tq = q_ref.shape[0]
    slab = kt_ref.shape[2]
    blk = ATT_TILE
    nblk = slab // blk
    first = lax.broadcasted_iota(jnp.int32, (tq, LANES), 1) < HEAD_DIM
    q = q_ref[...]
    zero = jnp.zeros_like(q)
    q_heads = (jnp.where(first, q, zero), jnp.where(first, zero, q))
    biases = (bias_ref[2 * hp], bias_ref[2 * hp + 1])
    suffix = _suffix_ones(blk)

    def score(s, mask):
        kt = kt_ref[s]
        z = jnp.concatenate([_dot(q_heads[h], kt) + biases[h] for h in range(2)], axis=0)
        drop = _log_drop(z)
        if mask is not None:
            drop = jnp.where(mask, drop, 0.0)
            z = jnp.where(mask, z, -jnp.inf)
        hi, lo = _split2(drop)
        return z, hi, lo

    def fold(s, scored, carry):
        z, hi, lo = scored
        acc, run = carry
        a_blocks = [None] * nblk
        for b in reversed(range(nblk)):
            cols = slice(b * blk, (b + 1) * blk)
            csum = _dot(jnp.concatenate([hi[:, cols], lo[:, cols]], axis=1), suffix)
            a_blocks[b] = jnp.exp(z[:, cols] - csum - run).astype(BF16)
            run = run + csum[:, 0:1]
        acc = acc + _dot_nt(jnp.concatenate(a_blocks, axis=1), vt_ref[s])
        return acc, run

    for n in range(nq):
        @pl.when(i == n)
        def _(n=n):
            row = lax.broadcasted_iota(jnp.int32, (2 * tq, slab), 0) & (tq - 1)
            col = lax.broadcasted_iota(jnp.int32, (2 * tq, slab), 1)
            pending = score(n, col < row)
            carry = (jnp.zeros((2 * tq, LANES), F32), jnp.zeros((2 * tq, 1), F32))
            for s in range(n - 1, -1, -1):
                scored = score(s, None)
                carry = fold(s + 1, pending, carry)
                pending = scored
            acc, _ = fold(0, pending, carry)
            o_ref[...] = jnp.where(first, acc[:tq], acc[tq:]).astype(o_ref.dtype)


def _sb_prompt(q, kt_slabs, vt_slabs, bias, batch):
    m, d_att = q.shape
    n_slabs, slab = kt_slabs.shape[1], kt_slabs.shape[3]
    tq = slab
    nq = m // batch // tq
    assert tq & (tq - 1) == 0 and slab % ATT_TILE == 0
    kv_spec = pl.BlockSpec((None, n_slabs, LANES, slab), lambda b, hp, i: (b, 0, hp, 0))
    return pl.pallas_call(
        functools.partial(_sb_prompt_body, nq=nq),
        grid=(batch, d_att // LANES, nq),
        in_specs=[
            pl.BlockSpec(memory_space=pltpu.SMEM),
            pl.BlockSpec((tq, LANES), lambda b, hp, i: (b * nq + i, hp)),
            kv_spec, kv_spec,
        ],
        out_specs=pl.BlockSpec((tq, LANES), lambda b, hp, i: (b * nq + i, hp)),
        out_shape=jax.ShapeDtypeStruct((m, d_att), BF16),
        compiler_params=_params("parallel", "parallel", "arbitrary"),
        name="sb_prompt",
    )(bias, q, kt_slabs, vt_slabs)


def _sb_sample_body(pt_ref, qb_ref, bias_ref, *refs, n_steps, pages):
    k_refs, v_refs = refs[:pages], refs[pages:2 * pages]
    o_ref, run_ref, acc_ref, z_ref, a_ref = refs[2 * pages:]
    j = pl.program_id(1)
    n_heads = qb_ref.shape[0]

    @pl.when(j == 0)
    def _():
        run_ref[...] = jnp.zeros(run_ref.shape, F32)
        acc_ref[...] = jnp.zeros(acc_ref.shape, F32)

    for h in range(n_heads):
        qh = qb_ref[h]
        for p in range(pages):
            z_ref[p, h:h + 1, :] = jnp.sum(k_refs[p][h] * qh, axis=0, keepdims=True)

    suffix = _suffix_ones(PAGE_SIZE)
    run = run_ref[...]
    for p in range(pages):
        z = z_ref[p] + bias_ref[...]
        hi, lo = _split2(_log_drop(z))
        csum = _dot(jnp.concatenate([hi, lo], axis=1), suffix)
        a_ref[p] = jnp.exp(z - csum - run)
        run = run + csum[:, 0:1]
    run_ref[...] = run

    for h in range(n_heads):
        acc = acc_ref[h]
        for p in range(pages):
            acc = acc + a_ref[p, h:h + 1, :] * v_refs[p][h]
        acc_ref[h] = acc

    @pl.when(j == n_steps - 1)
    def _():
        rows = acc_ref.shape[0] * acc_ref.shape[1]
        o_ref[...] = jnp.sum(acc_ref[...].reshape(rows, acc_ref.shape[2]), axis=1, keepdims=True)


def _sb_sample(q_lanes, bias_lanes, cache_kt, cache_vt, page_table, layer):
    batch, n_heads, dim, ps = q_lanes.shape
    n_pages = page_table.shape[1]
    pages = SAMPLE_PAGES_PER_STEP
    n_steps = n_pages // pages

    def page_spec(p):
        return pl.BlockSpec((None, None, n_heads, dim, ps),
                            lambda b, j, pt: (layer, pt[b, n_pages - 1 - (j * pages + p)], 0, 0, 0))

    page_specs = [page_spec(p) for p in range(pages)]
    return pl.pallas_call(
        functools.partial(_sb_sample_body, n_steps=n_steps, pages=pages),
        grid_spec=pltpu.PrefetchScalarGridSpec(
            num_scalar_prefetch=1,
            grid=(batch, n_steps),
            in_specs=[pl.BlockSpec((None, n_heads, dim, ps), lambda b, j, pt: (b, 0, 0, 0)),
                      pl.BlockSpec(bias_lanes.shape, lambda b, j, pt: (0, 0))] + page_specs + page_specs,
            out_specs=pl.BlockSpec((None, n_heads * dim, 1), lambda b, j, pt: (b, 0, 0)),
            scratch_shapes=[pltpu.VMEM((n_heads, ps), F32), pltpu.VMEM((n_heads, dim, ps), F32),
                            pltpu.VMEM((pages, n_heads, ps), F32), pltpu.VMEM((pages, n_heads, ps), F32)],
        ),
        out_shape=jax.ShapeDtypeStruct((batch, n_heads * dim, 1), F32),
        compiler_params=_params("arbitrary", "arbitrary"),
        name="sb_sample",
    )(page_table, q_lanes, bias_lanes, *([cache_kt] * pages), *([cache_vt] * pages))


def _pad_lanes(a):
    return jnp.pad(a, ((0, 0), (0, LANES - a.shape[1])))[:, None, :]


def kernel(x_prompt, x_sample, p_prompt, p_sample, cache_k, cache_v, page_table, state_conv, state_ssm, ffn1_pre_g, ffn1_w_gu, ffn1_w_down, ffn1_post_g, mix_pre_g, w_in, conv_w, conv_b, dt_bias, a_log, d_skip, ssm_norm_g, sb_bias, w_out, mix_post_g, ffn2_pre_g, ffn2_w_gu, ffn2_w_down, ffn2_post_g, ple_norm_g, w_ple_gate, w_ple_proj):
    depth = w_in.shape[0]
    bp, seq, d_model = x_prompt.shape
    bs = x_sample.shape[0]
    d_ssm = ssm_norm_g.shape[1]
    n_ssm_heads = d_skip.shape[1]
    n_att_heads = sb_bias.shape[1]
    d_att = n_att_heads * HEAD_DIM
    d_bc = 2 * N_BC_GROUPS * D_STATE
    conv_dim = d_ssm + d_bc
    n_pool = cache_k.shape[1]

    off_xbc = d_ssm
    off_dt = off_xbc + conv_dim
    off_q = off_dt + n_ssm_heads
    off_k = off_q + d_att
    off_v = off_k + d_att

    assert d_att == WIDE_TILE and d_ssm == WIDE_TILE
    w_main = jnp.concatenate(
        [w_in[:, :, off_q:], w_in[:, :, :off_xbc], w_in[:, :, off_xbc:off_xbc + d_ssm]], axis=2).astype(BF16)
    w_extra = jnp.pad(w_in[:, :, off_xbc + d_ssm:off_q],
                      ((0, 0), (0, 0), (0, LANES - n_ssm_heads))).astype(BF16)
    w_gu1, w_dn1 = ffn1_w_gu.astype(BF16), ffn1_w_down.astype(BF16)
    w_gu2, w_dn2 = ffn2_w_gu.astype(BF16), ffn2_w_down.astype(BF16)
    w_o = w_out.astype(BF16)
    w_pg, w_pp = w_ple_gate.astype(BF16), w_ple_proj.astype(BF16)

    row3 = lambda g: g[:, None, :]
    g_f1a, g_f1b = row3(ffn1_pre_g), row3(ffn1_post_g)
    g_f2a, g_f2b = row3(ffn2_pre_g), row3(ffn2_post_g)
    g_mixa, g_mixb = row3(mix_pre_g), row3(mix_post_g)
    g_ple = row3(ple_norm_g)
    dtb, alog = _pad_lanes(dt_bias), _pad_lanes(a_log)
    dskip = jnp.repeat(d_skip, HEAD_DIM, axis=1)[:, None, :]
    ng = row3(ssm_norm_g)
    bias_lanes = jnp.broadcast_to(sb_bias[:, :, None], (depth, n_att_heads, PAGE_SIZE))

    cache_kt = jnp.transpose(cache_k, (0, 1, 3, 4, 2))
    cache_vt = jnp.transpose(cache_v, (0, 1, 3, 4, 2))
    ssm_rows = state_ssm.reshape(depth, bs, n_ssm_heads * HEAD_DIM, D_STATE)

    mp = bp * seq
    hp = x_prompt.reshape(mp, d_model)
    hs = jnp.pad(x_sample.reshape(bs, d_model), ((0, SAMPLE_ROWS - bs), (0, 0)))
    pp = p_prompt.reshape(depth, mp, p_prompt.shape[3])
    psm = jnp.pad(p_sample.reshape(depth, bs, p_sample.shape[3]), ((0, 0), (0, SAMPLE_ROWS - bs), (0, 0)))

    outs = {name: [] for name in ("kp", "vp", "cp", "sp", "ks", "vs", "cs", "ss")}
    for i in range(depth):
        hp = _ffn(hp, g_f1a, w_gu1, w_dn1, g_f1b, i, ROW_TILE)
        q, kt, vt, kt_slabs, vt_slabs, z, x, bc, dt = _proj_cols(hp, g_mixa, w_main, w_extra, i, bp)
        y, hfin, tail_x, tail_bc = _ssd_prompt(
            x, bc, dt, z,
            conv_w[i][:, :d_ssm], conv_b[i][None, :d_ssm], conv_w[i][:, d_ssm:], conv_b[i][None, d_ssm:],
            dtb[i], alog[i], dskip[i], ng[i], bp)
        o = _sb_prompt(q, kt_slabs, vt_slabs, sb_bias[i], bp)
        hp = _outproj(hp, y, o, w_o, g_mixb, i, RESIDENT_ROW_TILE)
        hp = _ffn(hp, g_f2a, w_gu2, w_dn2, g_f2b, i, ROW_TILE)
        hp = _ple(hp, pp, g_ple, w_pg, w_pp, i, RESIDENT_ROW_TILE)
        keep = SUBLANES - (CONV_W - 1)
        token_major = lambda t: jnp.transpose(t.reshape(bp, n_att_heads, HEAD_DIM, seq), (0, 3, 1, 2))
        outs["kp"].append(token_major(kt))
        outs["vp"].append(token_major(vt))
        outs["cp"].append(jnp.concatenate([tail_x[:, keep:], tail_bc[:, keep:]], axis=2))
        outs["sp"].append(hfin.reshape(bp, n_ssm_heads, HEAD_DIM, D_STATE))

        hs = _ffn(hs, g_f1a, w_gu1, w_dn1, g_f1b, i, SAMPLE_ROWS)
        q, k, v, z, x, bc, dt = _proj_rows(hs, g_mixa, w_main, w_extra, i)
        y, conv_new, ssm_new = _ssd_sample(
            x, bc, dt, z, state_conv, ssm_rows, conv_w[i], conv_b[i][None, :],
            dtb[i], alog[i], dskip[i], ng[i], i, bs)
        q_lanes = jnp.broadcast_to(q[:bs].reshape(bs, n_att_heads, HEAD_DIM, 1),
                                   (bs, n_att_heads, HEAD_DIM, PAGE_SIZE))
        o = _sb_sample(q_lanes, bias_lanes[i], cache_kt, cache_vt, page_table, i)
        pad_rows = lambda a: jnp.pad(a.reshape(bs, -1), ((0, SAMPLE_ROWS - bs), (0, 0)))
        hs = _outproj(hs, pad_rows(y), pad_rows(o), w_o, g_mixb, i, SAMPLE_ROWS)
        hs = _ffn(hs, g_f2a, w_gu2, w_dn2, g_f2b, i, SAMPLE_ROWS)
        hs = _ple(hs, psm, g_ple, w_pg, w_pp, i, SAMPLE_ROWS)
        outs["ks"].append(k[:bs].reshape(bs, 1, n_att_heads, HEAD_DIM))
        outs["vs"].append(v[:bs].reshape(bs, 1, n_att_heads, HEAD_DIM))
        outs["cs"].append(conv_new)
        outs["ss"].append(ssm_new.reshape(bs, n_ssm_heads, HEAD_DIM, D_STATE))

    stack = lambda name: jnp.stack(outs[name])
    return (hp.reshape(bp, seq, d_model), hs[:bs].reshape(bs, 1, d_model),
            stack("kp"), stack("vp"), stack("cp"), stack("sp"),
            stack("ks"), stack("vs"), stack("cs"), stack("ss"))
```
